```python
import math
import jax
import jax.numpy as jnp
from jax import lax
import numpy as np

D_MODEL = 1024
BATCH = 2
SEQ = 8192
DEPTH = 4
DEC_BATCH = 32
DEC_SEQ = 4
PAST_LEN = 8192
PAGE_SIZE = 128

N_HEADS_A = 8
HEAD_DIM_A = 64
W_A = N_HEADS_A * HEAD_DIM_A
Q_BLOCK = 128
FORGET_BIAS_INIT = 3.0
W_B = 512
CONV_WIDTH = 3
W_C = 512
CHUNK = 128
N_GROUPS_C = 4
GROUP_DIM_C = W_C // N_GROUPS_C
N_BRANCH = 3
ALPHA = (2.0 * DEPTH) ** 0.25
BETA = (8.0 * DEPTH) ** -0.25
LN_EPS = 1e-5
SIZES = (W_A, W_A, W_A, N_HEADS_A, W_A, W_B, W_B, W_B, W_B, W_C, W_C, W_C, D_MODEL, D_MODEL, D_MODEL)
P_TOTAL = 4 * W_A + N_HEADS_A + 4 * W_B + 3 * W_C + N_BRANCH * D_MODEL

kernel_name = "fox_shortconv_chunkmlp_hybrid_step"


def _layer_norm(x, g, b):
    xf = x.astype(jnp.float32)
    mu = jnp.mean(xf, axis=-1, keepdims=True)
    var = jnp.mean(jnp.square(xf - mu), axis=-1, keepdims=True)
    y = (xf - mu) * lax.rsqrt(var + LN_EPS)
    return (y * g.astype(jnp.float32) + b.astype(jnp.float32)).astype(x.dtype)


def _project(h, w_in_l, b_f_l):
    p = h @ w_in_l
    cuts = np.cumsum(np.array(SIZES))[:-1].tolist()
    (q, k, v, f, ga, cb, cc, ch, gb, u, vc, gc, ma, mb, mc) = jnp.split(p, cuts, axis=-1)
    heads = h.shape[:-1] + (N_HEADS_A, HEAD_DIM_A)
    logf = jax.nn.log_sigmoid(f.astype(jnp.float32) + b_f_l.astype(jnp.float32))
    attn = (q.reshape(heads), k.reshape(heads), v.reshape(heads), logf, ga)
    conv = (cb, cc, ch, gb)
    chunk = (u, vc, gc)
    merge = (ma, mb, mc)
    return attn, conv, chunk, merge


def _fox_prompt(q, k, v, logf):
    b, s, _, _ = q.shape
    nblk = s // Q_BLOCK
    c = jnp.cumsum(logf, axis=1)
    cT = jnp.swapaxes(c, 1, 2)
    qb = jnp.moveaxis(q.reshape(b, nblk, Q_BLOCK, N_HEADS_A, HEAD_DIM_A), 1, 0)
    cqb = jnp.moveaxis(cT.reshape(b, N_HEADS_A, nblk, Q_BLOCK), 2, 0)
    kpos = jnp.arange(s)
    scale = HEAD_DIM_A ** -0.5

    def block(args):
        i, qi, cqi = args
        sc = jnp.einsum('bqhd,bkhd->bhqk', qi, k).astype(jnp.float32) * scale
        sc = sc + cqi[..., None] - cT[:, :, None, :]
        qpos = i * Q_BLOCK + jnp.arange(Q_BLOCK)
        mask = kpos[None, :] <= qpos[:, None]
        sc = jnp.where(mask, sc, -jnp.inf)
        p = jax.nn.softmax(sc, axis=-1).astype(v.dtype)
        return jnp.einsum('bhqk,bkhd->bqhd', p, v)

    out = lax.map(block, (jnp.arange(nblk), qb, cqb))
    return jnp.moveaxis(out, 0, 1).reshape(b, s, W_A)


def _fox_sample(q, k, v, logf, cache_k, cache_v, cache_logf, layer, page_table):
    nb, t = q.shape[:2]
    kp = cache_k[layer, page_table].reshape(nb, -1, N_HEADS_A, HEAD_DIM_A)
    vp = cache_v[layer, page_table].reshape(nb, -1, N_HEADS_A, HEAD_DIM_A)
    lp = cache_logf[layer, page_table].reshape(nb, -1, N_HEADS_A)
    past = kp.shape[1]
    k_all = jnp.concatenate([kp.astype(k.dtype), k], axis=1)
    v_all = jnp.concatenate([vp.astype(v.dtype), v], axis=1)
    c = jnp.cumsum(jnp.concatenate([lp.astype(jnp.float32), logf], axis=1), axis=1)
    cT = jnp.swapaxes(c, 1, 2)
    sc = jnp.einsum('bqhd,bkhd->bhqk', q, k_all).astype(jnp.float32) * HEAD_DIM_A ** -0.5
    sc = sc + cT[:, :, past:, None] - cT[:, :, None, :]
    mask = jnp.arange(past + t)[None, :] <= (past + jnp.arange(t))[:, None]
    sc = jnp.where(mask, sc, -jnp.inf)
    p = jax.nn.softmax(sc, axis=-1).astype(v_all.dtype)
    o = jnp.einsum('bhqk,bkhd->bqhd', p, v_all)
    return o.reshape(nb, t, W_A)


def _short_conv(cb, cc, ch, w, buf):
    z = cc * ch
    t = z.shape[1]
    zc = jnp.concatenate([buf.astype(z.dtype), z], axis=1)
    y = w[0] * zc[:, 0:t]
    for j in range(1, CONV_WIDTH):
        y = y + w[j] * zc[:, j:j + t]
    return cb * y, zc[:, t:]


def _chunk_mix(u, v, g, b, ws, bs):
    n_seq, L, _ = v.shape
    n = min(L, CHUNK)
    vn = _layer_norm(v, g, b)
    vr = vn.reshape(n_seq, L // n, n, N_GROUPS_C, GROUP_DIM_C)
    wm = jnp.tril(ws[:, :n, :n])
    s = jnp.einsum('gts,bcsgd->bctgd', wm, vr) + jnp.swapaxes(bs[:, :n], 0, 1)[:, :, None]
    return u * s.reshape(n_seq, L, W_C), vn


def _merge(h, ya, yb, yc, ga, gb, gc, ma, mb, mc, wa, wb, wc, wo, g, b):
    oa = (ya * jax.nn.silu(ga)) @ wa
    ob = (yb * jax.nn.silu(gb)) @ wb
    oc = (yc * jax.nn.silu(gc)) @ wc
    m = jax.nn.sigmoid(ma) * oa + jax.nn.sigmoid(mb) * ob + jax.nn.sigmoid(mc) * oc
    return _layer_norm(ALPHA * h + m @ wo, g, b)


def setup_inputs(seed: int = 0) -> dict:
    key = jax.random.key(seed)
    ks = jax.random.split(key, 24)
    f32 = jnp.float32
    n_pages = PAST_LEN // PAGE_SIZE
    n_pool = (DEC_BATCH * n_pages * 5) // 4

    def nrm(k, shape, s):
        return jax.random.normal(k, shape, f32) * s

    x_prompt = nrm(ks[0], (BATCH, SEQ, D_MODEL), 1.0)
    x_sample = nrm(ks[1], (DEC_BATCH, DEC_SEQ, D_MODEL), 1.0)
    cache_k = nrm(ks[2], (DEPTH, n_pool, PAGE_SIZE, N_HEADS_A, HEAD_DIM_A), 1.0)
    cache_v = nrm(ks[3], (DEPTH, n_pool, PAGE_SIZE, N_HEADS_A, HEAD_DIM_A), 1.0)
    cache_logf = jax.nn.log_sigmoid(FORGET_BIAS_INIT + nrm(ks[4], (DEPTH, n_pool, PAGE_SIZE, N_HEADS_A), 1.0))
    state_conv = nrm(ks[5], (DEPTH, DEC_BATCH, CONV_WIDTH - 1, W_B), 1.0)
    page_table = jax.random.permutation(ks[6], n_pool)[:DEC_BATCH * n_pages].reshape(DEC_BATCH, n_pages).astype(jnp.int32)
    ln_in_g = 1.0 + nrm(ks[7], (D_MODEL,), 0.02)
    ln_in_b = nrm(ks[8], (D_MODEL,), 0.02)
    w_in = nrm(ks[9], (DEPTH, D_MODEL, P_TOTAL), D_MODEL ** -0.5)
    b_f = FORGET_BIAS_INIT + nrm(ks[10], (DEPTH, N_HEADS_A), 0.5)
    conv_w = nrm(ks[11], (DEPTH, CONV_WIDTH, W_B), CONV_WIDTH ** -0.5)
    ln_v_g = 1.0 + nrm(ks[12], (DEPTH, W_C), 0.02)
    ln_v_b = nrm(ks[13], (DEPTH, W_C), 0.02)
    w_s = nrm(ks[14], (DEPTH, N_GROUPS_C, CHUNK, CHUNK), CHUNK ** -0.5)
    b_s = 1.0 + nrm(ks[15], (DEPTH, N_GROUPS_C, CHUNK), 0.02)
    w_a_out = nrm(ks[16], (DEPTH, W_A, D_MODEL), BETA * W_A ** -0.5)
    w_b_out = nrm(ks[17], (DEPTH, W_B, D_MODEL), BETA * W_B ** -0.5)
    w_c_out = nrm(ks[18], (DEPTH, W_C, D_MODEL), BETA * W_C ** -0.5)
    w_o = nrm(ks[19], (DEPTH, D_MODEL, D_MODEL), BETA * D_MODEL ** -0.5)
    ln_g = 1.0 + nrm(ks[20], (DEPTH, D_MODEL), 0.02)
    ln_b = nrm(ks[21], (DEPTH, D_MODEL), 0.02)
    return {"x_prompt": x_prompt, "x_sample": x_sample, "cache_k": cache_k, "cache_v": cache_v,
            "cache_logf": cache_logf, "state_conv": state_conv, "page_table": page_table,
            "ln_in_g": ln_in_g, "ln_in_b": ln_in_b, "w_in": w_in, "b_f": b_f, "conv_w": conv_w,
            "ln_v_g": ln_v_g, "ln_v_b": ln_v_b, "w_s": w_s, "b_s": b_s, "w_a_out": w_a_out,
            "w_b_out": w_b_out, "w_c_out": w_c_out, "w_o": w_o, "ln_g": ln_g, "ln_b": ln_b}


def reference(x_prompt, x_sample, cache_k, cache_v, cache_logf, state_conv, page_table,
              ln_in_g, ln_in_b, w_in, b_f, conv_w, ln_v_g, ln_v_b, w_s, b_s,
              w_a_out, w_b_out, w_c_out, w_o, ln_g, ln_b):
    hp = _layer_norm(x_prompt, ln_in_g, ln_in_b)
    hs = _layer_norm(x_sample, ln_in_g, ln_in_b)
    kp_l, vp_l, lp_l, cp_l = [], [], [], []
    ks_l, vs_l, ls_l, cs_l, us_l = [], [], [], [], []
    for l in range(DEPTH):
        (q, k, v, logf, ga), (cb, cc, ch, gb), (u, vc, gc), (ma, mb, mc) = _project(hp, w_in[l], b_f[l])
        ya = _fox_prompt(q, k, v, logf)
        zero_buf = jnp.zeros((hp.shape[0], CONV_WIDTH - 1, W_B), hp.dtype)
        yb, conv_tail = _short_conv(cb, cc, ch, conv_w[l], zero_buf)
        yc, _ = _chunk_mix(u, vc, ln_v_g[l], ln_v_b[l], w_s[l], b_s[l])
        kp_l.append(k); vp_l.append(v); lp_l.append(logf); cp_l.append(conv_tail)
        hp_next = _merge(hp, ya, yb, yc, ga, gb, gc, ma, mb, mc,
                         w_a_out[l], w_b_out[l], w_c_out[l], w_o[l], ln_g[l], ln_b[l])
        (q, k, v, logf, ga), (cb, cc, ch, gb), (u, vc, gc), (ma, mb, mc) = _project(hs, w_in[l], b_f[l])
        ya = _fox_sample(q, k, v, logf, cache_k, cache_v, cache_logf, l, page_table)
        yb, conv_tail = _short_conv(cb, cc, ch, conv_w[l], state_conv[l])
        yc, vn = _chunk_mix(u, vc, ln_v_g[l], ln_v_b[l], w_s[l], b_s[l])
        ks_l.append(k); vs_l.append(v); ls_l.append(logf); cs_l.append(conv_tail); us_l.append(vn)
        hs_next = _merge(hs, ya, yb, yc, ga, gb, gc, ma, mb, mc,
                         w_a_out[l], w_b_out[l], w_c_out[l], w_o[l], ln_g[l], ln_b[l])
        hp, hs = hp_next, hs_next
    new_k_prompt = jnp.stack(kp_l)
    new_v_prompt = jnp.stack(vp_l)
    new_logf_prompt = jnp.stack(lp_l)
    new_conv_prompt = jnp.stack(cp_l)
    new_k_sample = jnp.stack(ks_l)
    new_v_sample = jnp.stack(vs_l)
    new_logf_sample = jnp.stack(ls_l)
    new_conv_sample = jnp.stack(cs_l)
    new_chunk_v_sample = jnp.stack(us_l)
    return (hp, hs, new_k_prompt, new_v_prompt, new_logf_prompt, new_conv_prompt,
            new_k_sample, new_v_sample, new_logf_sample, new_conv_sample, new_chunk_v_sample)
```

```python
import functools

import numpy as np
import jax
import jax.numpy as jnp
from jax import lax
from jax.experimental import pallas as pl
from jax.experimental.pallas import tpu as pltpu

F32 = jnp.float32
BF16 = jnp.bfloat16

N_HEADS = 8
HEAD_DIM = 64
BRANCH_W = 512
CHUNK = 128
N_GROUPS = 4
GROUP_DIM = BRANCH_W // N_GROUPS
CONV_WIDTH = 3
LN_EPS = 1e-5
LANES = 128
SUBLANES = 8
FORGET_PAD = LANES
N_BRANCH_SLOTS = 11
SCALE = HEAD_DIM ** -0.5
VMEM_LIMIT = 56 * 1024 * 1024

AUG0 = HEAD_DIM


def _offsets(d_model):
    names = ("q", "k", "v", "ga", "cb", "cc", "ch", "gb", "u", "vc", "gc")
    off = {n: i * BRANCH_W for i, n in enumerate(names)}
    base = N_BRANCH_SLOTS * BRANCH_W
    off["ma"], off["mb"], off["mc"] = base, base + d_model, base + 2 * d_model
    off["f"] = base + 3 * d_model
    return off, base + 3 * d_model + FORGET_PAD


def _dot(a, b):
    return jnp.dot(a, b, preferred_element_type=F32)


def _dot_nt(a, b):
    return lax.dot_general(a, b, (((1,), (1,)), ((), ())), preferred_element_type=F32)


def _split3(x):
    hi = x.astype(BF16)
    r = x - hi.astype(F32)
    mid = r.astype(BF16)
    lo = (r - mid.astype(F32)).astype(BF16)
    return hi, mid, lo


def _sigmoid(x):
    return 1.0 / (1.0 + jnp.exp(-x))


def _silu(x):
    return x * _sigmoid(x)


def _log_sigmoid(x):
    return jnp.minimum(x, 0.0) - jnp.log1p(jnp.exp(-jnp.abs(x)))


def _layer_norm(x, g, b):
    mu = jnp.mean(x, axis=-1, keepdims=True)
    xc = x - mu
    var = jnp.mean(xc * xc, axis=-1, keepdims=True)
    return xc * lax.rsqrt(var + LN_EPS) * g + b


def _head_tiles(x):
    out = []
    for j in range(BRANCH_W // LANES):
        blk = x[:, j * LANES:(j + 1) * LANES]
        out.append(blk)
        out.append(pltpu.roll(blk, HEAD_DIM, axis=1))
    return out


def _ln_kernel(x_ref, g_ref, b_ref, o_ref):
    o_ref[...] = _layer_norm(x_ref[...], g_ref[...], b_ref[...])


def _ln_rows(x, g, b, tm):
    n, d = x.shape
    return pl.pallas_call(
        _ln_kernel,
        grid=(n // tm,),
        in_specs=[pl.BlockSpec((tm, d), lambda i: (i, 0)),
                  pl.BlockSpec((1, d), lambda i: (0, 0)),
                  pl.BlockSpec((1, d), lambda i: (0, 0))],
        out_specs=pl.BlockSpec((tm, d), lambda i: (i, 0)),
        out_shape=jax.ShapeDtypeStruct((n, d), F32),
        name="ln_in",
    )(x, g.reshape(1, d), b.reshape(1, d))


def _conv_gate(z, z1, z2, cb, gb, cw_ref):
    y = cw_ref[0:1, :] * z2 + cw_ref[1:2, :] * z1 + cw_ref[2:3, :] * z
    return (cb * y * _silu(gb)).astype(BF16)


def _prompt_proj_kernel(h_ref, w1_ref, bf_ref, cw_ref, lvg_ref, lvb_ref, ws_ref, bst_ref, wb_ref, wc_ref,
                        qa_ref, ka_ref, va_ref, ko_ref, vo_ref, lf_ref, tail_ref, sga_ref, sma_ref, mbc_ref,
                        ccar_ref, zbuf_ref, *, tm, d_model):
    off, _ = _offsets(d_model)
    i = pl.program_id(1)

    @pl.when(i == 0)
    def _():
        ccar_ref[...] = jnp.zeros_like(ccar_ref)
        zbuf_ref[0:SUBLANES, :] = jnp.zeros((SUBLANES, BRANCH_W), F32)

    hb = h_ref[0].astype(BF16)

    def proj(name, width=BRANCH_W):
        return _dot(hb, w1_ref[:, off[name]:off[name] + width])

    lane = lax.broadcasted_iota(jnp.int32, (tm, LANES), 1)

    logf = _log_sigmoid(proj("f", FORGET_PAD) + bf_ref[...])
    lf_ref[0] = logf[:, :N_HEADS]
    tri = jnp.where(lax.broadcasted_iota(jnp.int32, (tm, tm), 0) >= lax.broadcasted_iota(jnp.int32, (tm, tm), 1),
                    1.0, 0.0).astype(BF16)
    c = ccar_ref[0:1, :]
    for piece in _split3(logf):
        c = c + _dot(tri, piece)
    ccar_ref[0:1, :] = c[tm - 1:tm, :]
    c3 = [p.astype(F32) for p in _split3(c)]

    q = proj("q") * SCALE
    k = proj("k")
    v = proj("v")
    ko_ref[0] = k
    vo_ref[0] = v
    qh, kh, vh = _head_tiles(q), _head_tiles(k), _head_tiles(v)
    for h in range(N_HEADS):
        cb3 = [jnp.broadcast_to(p[:, h:h + 1], (tm, LANES)) for p in c3]
        aq = jnp.where(lane == AUG0, cb3[0], jnp.where(lane == AUG0 + 1, cb3[1], jnp.where(
            lane == AUG0 + 2, cb3[2], jnp.where(lane < AUG0 + 6, 1.0, 0.0))))
        qa_ref[0, h] = jnp.where(lane < HEAD_DIM, qh[h], aq).astype(BF16)
        ak = jnp.where(lane < AUG0 + 3, 1.0, jnp.where(lane == AUG0 + 3, -cb3[0], jnp.where(
            lane == AUG0 + 4, -cb3[1], jnp.where(lane == AUG0 + 5, -cb3[2], 0.0))))
        ka_ref[0, h] = jnp.where(lane < HEAD_DIM, kh[h], ak).astype(BF16)
        va_ref[0, h] = jnp.where(lane < HEAD_DIM, vh[h], jnp.where(lane == AUG0, 1.0, 0.0)).astype(BF16)

    ga = proj("ga")
    sga_ref[0] = _silu(ga).astype(BF16)

    z = proj("cc") * proj("ch")
    zbuf_ref[SUBLANES:SUBLANES + tm, :] = z
    z1 = zbuf_ref[SUBLANES - 1:SUBLANES - 1 + tm, :]
    z2 = zbuf_ref[SUBLANES - 2:SUBLANES - 2 + tm, :]
    zbuf_ref[0:SUBLANES, :] = z[tm - SUBLANES:tm, :]
    tail_ref[0] = z[tm - (CONV_WIDTH - 1):tm, :]
    ob = _dot(_conv_gate(z, z1, z2, proj("cb"), proj("gb"), cw_ref), wb_ref[...])

    vn = _layer_norm(proj("vc"), lvg_ref[...], lvb_ref[...]).astype(BF16)
    tril = lax.broadcasted_iota(jnp.int32, (CHUNK, CHUNK), 0) >= lax.broadcasted_iota(jnp.int32, (CHUNK, CHUNK), 1)
    wm = [jnp.where(tril, ws_ref[g], 0.0).astype(BF16) for g in range(N_GROUPS)]
    rows = []
    for ci in range(tm // CHUNK):
        rows.append(jnp.concatenate(
            [_dot(wm[g], vn[ci * CHUNK:(ci + 1) * CHUNK, g * GROUP_DIM:(g + 1) * GROUP_DIM]) + bst_ref[:, g:g + 1]
             for g in range(N_GROUPS)], axis=1))
    s = jnp.concatenate(rows, axis=0)
    oc = _dot((proj("u") * s * _silu(proj("gc"))).astype(BF16), wc_ref[...])

    sma_ref[0] = _sigmoid(proj("ma", d_model)).astype(BF16)
    mbc_ref[0] = (_sigmoid(proj("mb", d_model)) * ob + _sigmoid(proj("mc", d_model)) * oc).astype(BF16)


def _prompt_proj(h, w1, bf, cw, lvg, lvb, ws, bst, wb, wc, tm):
    b, s, d = h.shape
    wtot = w1.shape[1]
    const = lambda shape: pl.BlockSpec(shape, lambda bi, i: (0,) * len(shape), pipeline_mode=pl.Buffered(1))
    rows = lambda w: pl.BlockSpec((1, tm, w), lambda bi, i: (bi, i, 0))
    heads = pl.BlockSpec((1, N_HEADS, tm, LANES), lambda bi, i: (bi, 0, i, 0))
    hshape = jax.ShapeDtypeStruct((b, N_HEADS, s, LANES), BF16)
    return pl.pallas_call(
        functools.partial(_prompt_proj_kernel, tm=tm, d_model=d),
        grid=(b, s // tm),
        in_specs=[rows(d), const((d, wtot)), const((1, FORGET_PAD)), const((CONV_WIDTH, BRANCH_W)),
                  const((1, BRANCH_W)), const((1, BRANCH_W)), const((N_GROUPS, CHUNK, CHUNK)),
                  const((CHUNK, N_GROUPS)), const((BRANCH_W, d)), const((BRANCH_W, d))],
        out_specs=[heads, heads, heads, rows(BRANCH_W), rows(BRANCH_W), rows(N_HEADS),
                   pl.BlockSpec((1, CONV_WIDTH - 1, BRANCH_W), lambda bi, i: (bi, 0, 0)),
                   rows(BRANCH_W), rows(d), rows(d)],
        out_shape=[hshape, hshape, hshape,
                   jax.ShapeDtypeStruct((b, s, BRANCH_W), F32), jax.ShapeDtypeStruct((b, s, BRANCH_W), F32),
                   jax.ShapeDtypeStruct((b, s, N_HEADS), F32),
                   jax.ShapeDtypeStruct((b, CONV_WIDTH - 1, BRANCH_W), F32),
                   jax.ShapeDtypeStruct((b, s, BRANCH_W), BF16), jax.ShapeDtypeStruct((b, s, d), BF16),
                   jax.ShapeDtypeStruct((b, s, d), BF16)],
        scratch_shapes=[pltpu.VMEM((SUBLANES, LANES), F32), pltpu.VMEM((tm + SUBLANES, BRANCH_W), F32)],
        compiler_params=pltpu.CompilerParams(dimension_semantics=("arbitrary", "arbitrary"),
                                             vmem_limit_bytes=VMEM_LIMIT),
        name="prompt_proj",
    )(h, w1, bf, cw, lvg, lvb, ws, bst, wb, wc)


def _flash_kernel(qi_ref, ki_ref, q_ref, k_ref, v_ref, o_ref, m_ref, acc_ref, *, tq, tk):
    p = pl.program_id(1)
    qi = qi_ref[p]
    ki = ki_ref[p]

    @pl.when(ki == 0)
    def _():
        m_ref[...] = jnp.full_like(m_ref, -jnp.inf)
        acc_ref[...] = jnp.zeros_like(acc_ref)

    qpos = qi * tq + lax.broadcasted_iota(jnp.int32, (tq, tk), 0)
    kpos = ki * tk + lax.broadcasted_iota(jnp.int32, (tq, tk), 1)
    mask = kpos <= qpos
    for h in range(N_HEADS):
        s = jnp.where(mask, _dot_nt(q_ref[0, h], k_ref[0, h]), -jnp.inf)
        m_old = m_ref[h]
        m_new = jnp.maximum(m_old, jnp.max(s, axis=1, keepdims=True))
        pexp = jnp.exp(s - m_new).astype(BF16)
        acc_ref[h] = jnp.exp(m_old - m_new) * acc_ref[h] + _dot(pexp, v_ref[0, h])
        m_ref[h] = m_new

    @pl.when(ki == (qi * tq + tq - 1) // tk)
    def _():
        lane = lax.broadcasted_iota(jnp.int32, (tq, LANES), 1)
        for j in range(N_HEADS // 2):
            a0, a1 = acc_ref[2 * j], acc_ref[2 * j + 1]
            y0 = a0 * (1.0 / a0[:, AUG0:AUG0 + 1])
            y1 = a1 * (1.0 / a1[:, AUG0:AUG0 + 1])
            o_ref[0, :, j * LANES:(j + 1) * LANES] = jnp.where(
                lane < HEAD_DIM, y0, pltpu.roll(y1, HEAD_DIM, axis=1)).astype(BF16)


def _flash(qa, ka, va, tq, tk):
    b, _, s, _ = qa.shape
    pairs = [(qi, ki) for qi in range(s // tq) for ki in range((qi * tq + tq - 1) // tk + 1)]
    qi_tab = jnp.asarray(np.array([p[0] for p in pairs], np.int32))
    ki_tab = jnp.asarray(np.array([p[1] for p in pairs], np.int32))
    grid_spec = pltpu.PrefetchScalarGridSpec(
        num_scalar_prefetch=2,
        grid=(b, len(pairs)),
        in_specs=[pl.BlockSpec((1, N_HEADS, tq, LANES), lambda bi, p, qt, kt: (bi, 0, qt[p], 0)),
                  pl.BlockSpec((1, N_HEADS, tk, LANES), lambda bi, p, qt, kt: (bi, 0, kt[p], 0)),
                  pl.BlockSpec((1, N_HEADS, tk, LANES), lambda bi, p, qt, kt: (bi, 0, kt[p], 0))],
        out_specs=pl.BlockSpec((1, tq, BRANCH_W), lambda bi, p, qt, kt: (bi, qt[p], 0)),
        scratch_shapes=[pltpu.VMEM((N_HEADS, tq, 1), F32), pltpu.VMEM((N_HEADS, tq, LANES), F32)],
    )
    return pl.pallas_call(
        functools.partial(_flash_kernel, tq=tq, tk=tk),
        grid_spec=grid_spec,
        out_shape=jax.ShapeDtypeStruct((b, s, BRANCH_W), BF16),
        compiler_params=pltpu.CompilerParams(dimension_semantics=("arbitrary", "arbitrary"),
                                             vmem_limit_bytes=VMEM_LIMIT),
        name="prompt_flash",
    )(qi_tab, ki_tab, qa, ka, va)


def _merge_kernel(h_ref, ya_ref, sga_ref, sma_ref, mbc_ref, wa_ref, wo_ref, g_ref, b_ref, o_ref, *, alpha):
    oa = _dot((ya_ref[...].astype(F32) * sga_ref[...].astype(F32)).astype(BF16), wa_ref[...])
    m = sma_ref[...].astype(F32) * oa + mbc_ref[...].astype(F32)
    x = alpha * h_ref[...] + _dot(m.astype(BF16), wo_ref[...])
    o_ref[...] = _layer_norm(x, g_ref[...], b_ref[...])


def _merge(h, ya, sga, sma, mbc, wa, wo, g, b, alpha, tm):
    n, d = h.shape
    rows = lambda w: pl.BlockSpec((tm, w), lambda i: (i, 0))
    const = lambda shape: pl.BlockSpec(shape, lambda i: (0,) * len(shape))
    return pl.pallas_call(
        functools.partial(_merge_kernel, alpha=alpha),
        grid=(n // tm,),
        in_specs=[rows(d), rows(BRANCH_W), rows(BRANCH_W), rows(d), rows(d),
                  const((BRANCH_W, d)), const((d, d)), const((1, d)), const((1, d))],
        out_specs=rows(d),
        out_shape=jax.ShapeDtypeStruct((n, d), F32),
        compiler_params=pltpu.CompilerParams(dimension_semantics=("arbitrary",), vmem_limit_bytes=VMEM_LIMIT),
        name="merge",
    )(h, ya, sga, sma, mbc, wa, wo, g, b)


def _sample_proj_kernel(h_ref, w1_ref, bf_ref, cw_ref, lvg_ref, lvb_ref, wst_ref, bsr_ref, wb_ref, wc_ref,
                        zp1_ref, zp2_ref,
                        q_ref, k_ref, v_ref, lf_ref, cnt_ref, z_ref, vn_ref, sga_ref, sma_ref, mbc_ref,
                        p_ref, zbuf_ref, *, tn, n_steps, t_new, d_model):
    off, _ = _offsets(d_model)
    j = pl.program_id(0)
    ns = h_ref.shape[0]
    col = pl.multiple_of(j * tn, LANES)
    p_ref[:, pl.ds(col, tn)] = _dot(h_ref[...].astype(BF16), w1_ref[...])

    @pl.when(j == n_steps - 1)
    def _():
        def proj(name, width=BRANCH_W):
            return p_ref[:, off[name]:off[name] + width]

        q_ref[...] = proj("q")
        k_ref[...] = proj("k")
        v_ref[...] = proj("v")
        logf = _log_sigmoid(proj("f", FORGET_PAD) + bf_ref[...])
        lf_ref[...] = logf[:, :N_HEADS]
        lt = logf.T
        tpos = lax.broadcasted_iota(jnp.int32, lt.shape, 1) % t_new
        cn = lt
        shift = 1
        while shift < t_new:
            cn = cn + jnp.where(tpos >= shift, pltpu.roll(cn, shift, axis=1), 0.0)
            shift *= 2
        cnt_ref[...] = cn[0:N_HEADS, :]

        sga_ref[...] = _silu(proj("ga")).astype(BF16)

        z = proj("cc") * proj("ch")
        z_ref[...] = z
        zbuf_ref[0:SUBLANES, :] = jnp.zeros((SUBLANES, BRANCH_W), F32)
        zbuf_ref[SUBLANES:SUBLANES + ns, :] = z
        trow = lax.broadcasted_iota(jnp.int32, (ns, BRANCH_W), 0) % t_new
        z1 = jnp.where(trow < 1, zp1_ref[...], zbuf_ref[SUBLANES - 1:SUBLANES - 1 + ns, :])
        z2 = jnp.where(trow < 2, zp2_ref[...], zbuf_ref[SUBLANES - 2:SUBLANES - 2 + ns, :])
        ob = _dot(_conv_gate(z, z1, z2, proj("cb"), proj("gb"), cw_ref), wb_ref[...])

        vn = _layer_norm(proj("vc"), lvg_ref[...], lvb_ref[...])
        vn_ref[...] = vn
        vnb = vn.astype(BF16)
        r = lax.broadcasted_iota(jnp.int32, (ns, ns), 0)
        c = lax.broadcasted_iota(jnp.int32, (ns, ns), 1)
        keep = (r // t_new == c // t_new) & (c % t_new <= r % t_new)
        s = jnp.concatenate(
            [_dot(jnp.where(keep, wst_ref[g], 0.0).astype(BF16), vnb[:, g * GROUP_DIM:(g + 1) * GROUP_DIM])
             + bsr_ref[:, g:g + 1] for g in range(N_GROUPS)], axis=1)
        oc = _dot((proj("u") * s * _silu(proj("gc"))).astype(BF16), wc_ref[...])

        sma_ref[...] = _sigmoid(proj("ma", d_model)).astype(BF16)
        mbc_ref[...] = (_sigmoid(proj("mb", d_model)) * ob + _sigmoid(proj("mc", d_model)) * oc).astype(BF16)


def _sample_proj(h, w1, bf, cw, lvg, lvb, wst, bsr, wb, wc, zp1, zp2, t_new):
    ns, d = h.shape
    wtot = w1.shape[1]
    n_steps = next(n for n in (3, 2, 1) if (wtot // LANES) % n == 0)
    tn = wtot // n_steps
    const = lambda shape: pl.BlockSpec(shape, lambda j: (0,) * len(shape))
    f32 = lambda w: jax.ShapeDtypeStruct((ns, w), F32)
    bf16 = lambda w: jax.ShapeDtypeStruct((ns, w), BF16)
    return pl.pallas_call(
        functools.partial(_sample_proj_kernel, tn=tn, n_steps=n_steps, t_new=t_new, d_model=d),
        grid=(n_steps,),
        in_specs=[const((ns, d)), pl.BlockSpec((d, tn), lambda j: (0, j)), const((1, FORGET_PAD)),
                  const((CONV_WIDTH, BRANCH_W)), const((1, BRANCH_W)), const((1, BRANCH_W)),
                  const((N_GROUPS, ns, ns)), const((ns, N_GROUPS)), const((BRANCH_W, d)), const((BRANCH_W, d)),
                  const((ns, BRANCH_W)), const((ns, BRANCH_W))],
        out_specs=[const((ns, BRANCH_W)), const((ns, BRANCH_W)), const((ns, BRANCH_W)), const((ns, N_HEADS)),
                   const((N_HEADS, ns)), const((ns, BRANCH_W)), const((ns, BRANCH_W)),
                   const((ns, BRANCH_W)), const((ns, d)), const((ns, d))],
        out_shape=[f32(BRANCH_W), f32(BRANCH_W), f32(BRANCH_W), f32(N_HEADS),
                   jax.ShapeDtypeStruct((N_HEADS, ns), F32), f32(BRANCH_W), f32(BRANCH_W),
                   bf16(BRANCH_W), bf16(d), bf16(d)],
        scratch_shapes=[pltpu.VMEM((ns, wtot), F32), pltpu.VMEM((ns + SUBLANES, BRANCH_W), F32)],
        compiler_params=pltpu.CompilerParams(dimension_semantics=("arbitrary",), vmem_limit_bytes=VMEM_LIMIT),
        name="sample_proj",
    )(h, w1, bf, cw, lvg, lvb, wst, bsr, wb, wc, zp1, zp2)


def _sample_attn_kernel(layer_ref, pt_ref, q_ref, kn_ref, vn_ref, cn_ref, *rest, pp, t_new):
    del layer_ref, pt_ref
    k_refs, v_refs, lf_refs = rest[0:pp], rest[pp:2 * pp], rest[2 * pp:3 * pp]
    o_ref, m_ref, l_ref, acc_ref, car_ref = rest[3 * pp:]
    step = pl.program_id(1)
    n_steps = pl.num_programs(1)
    tpad = q_ref.shape[1]
    trow = lax.broadcasted_iota(jnp.int32, (tpad, 1), 0)

    @pl.when(step == 0)
    def _():
        car_ref[...] = jnp.zeros_like(car_ref)
        for h in range(N_HEADS):
            qh = q_ref[0, :, h, :] * SCALE
            m = jnp.full((tpad, 1), -jnp.inf, F32)
            l = jnp.zeros((tpad, 1), F32)
            acc = jnp.zeros((tpad, HEAD_DIM), F32)
            for t in range(t_new):
                s = jnp.sum(qh * kn_ref[0, t:t + 1, h, :], axis=1, keepdims=True) - cn_ref[0, h:h + 1, t:t + 1]
                s = jnp.where(trow >= t, s, -jnp.inf)
                m_new = jnp.maximum(m, s)
                a = jnp.exp(m - m_new)
                p = jnp.exp(s - m_new)
                l = a * l + p
                acc = a * acc + p * vn_ref[0, t:t + 1, h, :]
                m = m_new
            m_ref[h] = m
            l_ref[h] = l
            acc_ref[h] = acc

    ci = lax.broadcasted_iota(jnp.int32, (LANES, 2 * LANES), 0)
    cj = lax.broadcasted_iota(jnp.int32, (LANES, 2 * LANES), 1)
    usum = jnp.where((cj >= LANES) | (ci > cj), 1.0, 0.0).astype(BF16)
    carry = car_ref[...]
    bias = []
    for i in range(pp):
        r = jnp.zeros((N_HEADS, 2 * LANES), F32)
        for piece in _split3(lf_refs[i][0, 0]):
            r = r + _dot(piece, usum)
        bias.append(r[:, :LANES] + carry)
        carry = carry + r[:, LANES:]
    car_ref[...] = carry
    bias = jnp.concatenate(bias, axis=1)

    for h in range(N_HEADS):
        qh = (q_ref[0, :, h, :] * SCALE).astype(BF16)
        kh = jnp.concatenate([k_refs[i][0, 0, :, h, :] for i in range(pp)], axis=0).astype(BF16)
        vh = jnp.concatenate([v_refs[i][0, 0, :, h, :] for i in range(pp)], axis=0).astype(BF16)
        s = _dot_nt(qh, kh) + bias[h:h + 1, :]
        m_old = m_ref[h]
        m_new = jnp.maximum(m_old, jnp.max(s, axis=1, keepdims=True))
        a = jnp.exp(m_old - m_new)
        p = jnp.exp(s - m_new)
        l_ref[h] = a * l_ref[h] + jnp.sum(p, axis=1, keepdims=True)
        acc_ref[h] = a * acc_ref[h] + _dot(p.astype(BF16), vh)
        m_ref[h] = m_new

    @pl.when(step == n_steps - 1)
    def _():
        for h in range(N_HEADS):
            o_ref[0, h] = acc_ref[h] * (1.0 / l_ref[h])


def _sample_attn(layer, page_table, q, kn, vn, cn, cache_k, cache_v, cache_lft, t_new):
    nb, tpad = q.shape[0], q.shape[1]
    n_pages = page_table.shape[1]
    page = cache_k.shape[2]
    pp = 8 if n_pages % 8 == 0 else n_pages
    n_steps = n_pages // pp

    def page_map(i):
        def index_map(b, s, lref, ptref):
            return (lref[0], ptref[b * n_pages + (n_pages - 1 - (s * pp + i))], 0, 0, 0)
        return index_map

    def lf_map(i):
        def index_map(b, s, lref, ptref):
            return (lref[0], ptref[b * n_pages + (n_pages - 1 - (s * pp + i))], 0, 0)
        return index_map

    per_seq = lambda shape: pl.BlockSpec((1,) + shape, lambda b, s, lref, ptref: (b,) + (0,) * len(shape))
    in_specs = [per_seq((tpad, N_HEADS, HEAD_DIM)), per_seq((tpad, N_HEADS, HEAD_DIM)),
                per_seq((tpad, N_HEADS, HEAD_DIM)), per_seq((N_HEADS, LANES))]
    in_specs += [pl.BlockSpec((1, 1, page, N_HEADS, HEAD_DIM), page_map(i)) for i in range(pp)]
    in_specs += [pl.BlockSpec((1, 1, page, N_HEADS, HEAD_DIM), page_map(i)) for i in range(pp)]
    in_specs += [pl.BlockSpec((1, 1, N_HEADS, page), lf_map(i)) for i in range(pp)]
    grid_spec = pltpu.PrefetchScalarGridSpec(
        num_scalar_prefetch=2,
        grid=(nb, n_steps),
        in_specs=in_specs,
        out_specs=per_seq((N_HEADS, tpad, HEAD_DIM)),
        scratch_shapes=[pltpu.VMEM((N_HEADS, tpad, 1), F32), pltpu.VMEM((N_HEADS, tpad, 1), F32),
                        pltpu.VMEM((N_HEADS, tpad, HEAD_DIM), F32), pltpu.VMEM((N_HEADS, LANES), F32)],
    )
    return pl.pallas_call(
        functools.partial(_sample_attn_kernel, pp=pp, t_new=t_new),
        grid_spec=grid_spec,
        out_shape=jax.ShapeDtypeStruct((nb, N_HEADS, tpad, HEAD_DIM), F32),
        compiler_params=pltpu.CompilerParams(dimension_semantics=("arbitrary", "arbitrary"),
                                             vmem_limit_bytes=VMEM_LIMIT),
        name="sample_attn",
    )(layer, page_table.reshape(-1), q, kn, vn, cn,
      *([cache_k] * pp), *([cache_v] * pp), *([cache_lft] * pp))


def _pick_tile(n, pref):
    t = pref
    while n % t:
        t //= 2
    return t


def kernel(x_prompt, x_sample, cache_k, cache_v, cache_logf, state_conv, page_table, ln_in_g, ln_in_b, w_in, b_f,
           conv_w, ln_v_g, ln_v_b, w_s, b_s, w_a_out, w_b_out, w_c_out, w_o, ln_g, ln_b):
    depth, d = w_in.shape[0], w_in.shape[1]
    bsz, seq, _ = x_prompt.shape
    nb, t_new, _ = x_sample.shape
    ns = nb * t_new
    assert seq % CHUNK == 0 and t_new <= SUBLANES and page_table.shape[0] == nb
    alpha = (2.0 * depth) ** 0.25
    tm = _pick_tile(seq, 256)
    tq = _pick_tile(seq, 512)

    f0 = 3 * BRANCH_W
    w1 = jnp.concatenate([w_in[..., :f0], w_in[..., f0 + N_HEADS:],
                          jnp.pad(w_in[..., f0:f0 + N_HEADS], ((0, 0), (0, 0), (0, FORGET_PAD - N_HEADS)))],
                         axis=-1).astype(BF16)
    bf = jnp.pad(b_f, ((0, 0), (0, FORGET_PAD - N_HEADS))).reshape(depth, 1, FORGET_PAD)
    wa, wb, wc, wo = (w.astype(BF16) for w in (w_a_out, w_b_out, w_c_out, w_o))
    bst = jnp.swapaxes(b_s, 1, 2)
    wst = jnp.tile(w_s[:, :, :t_new, :t_new], (1, 1, ns // t_new, ns // t_new))
    bsr = jnp.tile(jnp.swapaxes(b_s[:, :, :t_new], 1, 2), (1, ns // t_new, 1))
    cache_lft = jnp.swapaxes(cache_logf, 2, 3)
    zeros_tail = jnp.zeros((nb, t_new - 1, BRANCH_W), F32)
    zp1 = jnp.concatenate([state_conv[:, :, 1:2], jnp.broadcast_to(zeros_tail, (depth,) + zeros_tail.shape)], axis=2)
    zp2 = jnp.concatenate([state_conv, jnp.zeros((depth, nb, t_new - 2, BRANCH_W), F32)], axis=2)
    zp1, zp2 = zp1.reshape(depth, ns, BRANCH_W), zp2.reshape(depth, ns, BRANCH_W)

    hp = _ln_rows(x_prompt.reshape(bsz * seq, d), ln_in_g, ln_in_b, tm).reshape(bsz, seq, d)
    hs = _ln_rows(x_sample.reshape(ns, d), ln_in_g, ln_in_b, ns)

    outs = {n: [] for n in ("kp", "vp", "lp", "cp", "ks", "vs", "ls", "cs", "us")}
    for l in range(depth):
        row1 = lambda a: a[l].reshape(1, -1)
        qa, ka, va, ko, vo, lf, tail, sga, sma, mbc = _prompt_proj(
            hp, w1[l], bf[l], conv_w[l], row1(ln_v_g), row1(ln_v_b), w_s[l], bst[l], wb[l], wc[l], tm)
        ya = _flash(qa, ka, va, tq, tq)
        flat = lambda a: a.reshape(bsz * seq, a.shape[-1])
        hp = _merge(flat(hp), flat(ya), flat(sga), flat(sma), flat(mbc), wa[l], wo[l], row1(ln_g), row1(ln_b),
                    alpha, tm).reshape(bsz, seq, d)
        outs["kp"].append(ko); outs["vp"].append(vo); outs["lp"].append(lf); outs["cp"].append(tail)
        qs, ks, vs, lfs, cnt, zs, vns, sga, sma, mbc = _sample_proj(
            hs, w1[l], bf[l], conv_w[l], row1(ln_v_g), row1(ln_v_b), wst[l], bsr[l], wb[l], wc[l],
            zp1[l], zp2[l], t_new)
        pad_t = lambda a: jnp.pad(a.reshape(nb, t_new, N_HEADS, HEAD_DIM), ((0, 0), (0, SUBLANES - t_new), (0, 0), (0, 0)))
        cn = jnp.pad(jnp.transpose(cnt.reshape(N_HEADS, nb, t_new), (1, 0, 2)), ((0, 0), (0, 0), (0, LANES - t_new)))
        yas = _sample_attn(jnp.full((1,), l, jnp.int32), page_table, pad_t(qs), pad_t(ks), pad_t(vs), cn,
                           cache_k, cache_v, cache_lft, t_new)
        yas = jnp.transpose(yas[:, :, :t_new], (0, 2, 1, 3)).reshape(ns, BRANCH_W)
        hs = _merge(hs, yas.astype(BF16), sga, sma, mbc, wa[l], wo[l], row1(ln_g), row1(ln_b), alpha, ns)
        outs["ks"].append(ks); outs["vs"].append(vs); outs["ls"].append(lfs)
        outs["cs"].append(zs.reshape(nb, t_new, BRANCH_W)[:, t_new - (CONV_WIDTH - 1):])
        outs["us"].append(vns)

    st = {n: jnp.stack(v) for n, v in outs.items()}
    return (hp, hs.reshape(nb, t_new, d),
            st["kp"].reshape(depth, bsz, seq, N_HEADS, HEAD_DIM), st["vp"].reshape(depth, bsz, seq, N_HEADS, HEAD_DIM),
            st["lp"], st["cp"],
            st["ks"].reshape(depth, nb, t_new, N_HEADS, HEAD_DIM), st["vs"].reshape(depth, nb, t_new, N_HEADS, HEAD_DIM),
            st["ls"].reshape(depth, nb, t_new, N_HEADS), st["cs"], st["us"].reshape(depth, nb, t_new, BRANCH_W))
```

```python
import functools
import math

import numpy as np
import jax
import jax.numpy as jnp
from jax import lax
from jax.experimental import pallas as pl
from jax.experimental.pallas import tpu as pltpu

F32 = jnp.float32
BF16 = jnp.bfloat16

N_HEADS = 8
HEAD_DIM = 64
BRANCH_W = 512
CHUNK = 128
N_GROUPS = 4
GROUP_DIM = BRANCH_W // N_GROUPS
CONV_WIDTH = 3
LN_EPS = 1e-5
LANES = 128
SUBLANES = 8
BF16_ROWS = 16
FORGET_PAD = LANES
ROW_SLOTS = ("k", "ga", "cb", "cc", "ch", "gb", "u", "vc", "gc")
W1_ALIGN = 2048
W1T_ROWS = 2 * BRANCH_W + FORGET_PAD
SCALE = HEAD_DIM ** -0.5
LOG2E = math.log2(math.e)
VMEM_LIMIT = 56 * 1024 * 1024
PAGES_PER_STEP = 16

AUG0 = HEAD_DIM


def _offsets(d_model):
    off = {n: i * BRANCH_W for i, n in enumerate(ROW_SLOTS)}
    base = len(ROW_SLOTS) * BRANCH_W
    off["ma"], off["mb"], off["mc"] = base, base + d_model, base + 2 * d_model
    off["f"] = base + 3 * d_model
    used = base + 3 * d_model + FORGET_PAD
    return off, -(-used // W1_ALIGN) * W1_ALIGN


def _dot(a, b):
    return jnp.dot(a, b, preferred_element_type=F32)


def _dot_nt(a, b):
    return lax.dot_general(a, b, (((1,), (1,)), ((), ())), preferred_element_type=F32)


def _split3(x):
    hi = x.astype(BF16)
    r = x - hi.astype(F32)
    mid = r.astype(BF16)
    lo = (r - mid.astype(F32)).astype(BF16)
    return hi, mid, lo


def _sigmoid(x):
    return 1.0 / (1.0 + jnp.exp(-x))


def _silu(x):
    return x * _sigmoid(x)


def _log_sigmoid(x):
    return jnp.minimum(x, 0.0) - jnp.log1p(jnp.exp(-jnp.abs(x)))


def _layer_norm(x, g, b):
    mu = jnp.mean(x, axis=-1, keepdims=True)
    xc = x - mu
    var = jnp.mean(xc * xc, axis=-1, keepdims=True)
    return xc * lax.rsqrt(var + LN_EPS) * g + b


def _head_tiles(x):
    out = []
    for j in range(BRANCH_W // LANES):
        blk = x[:, j * LANES:(j + 1) * LANES]
        out.append(blk)
        out.append(pltpu.roll(blk, HEAD_DIM, axis=1))
    return out


def _ln_kernel(x_ref, g_ref, b_ref, o_ref):
    o_ref[...] = _layer_norm(x_ref[...], g_ref[...], b_ref[...])


def _ln_rows(x, g, b, tm):
    n, d = x.shape
    return pl.pallas_call(
        _ln_kernel,
        grid=(n // tm,),
        in_specs=[pl.BlockSpec((tm, d), lambda i: (i, 0)),
                  pl.BlockSpec((1, d), lambda i: (0, 0)),
                  pl.BlockSpec((1, d), lambda i: (0, 0))],
        out_specs=pl.BlockSpec((tm, d), lambda i: (i, 0)),
        out_shape=jax.ShapeDtypeStruct((n, d), F32),
        name="ln_in",
    )(x, g.reshape(1, d), b.reshape(1, d))


def _conv_gate(z, z1, z2, cb, gb, cw_ref):
    y = cw_ref[0:1, :] * z2 + cw_ref[1:2, :] * z1 + cw_ref[2:3, :] * z
    return (cb * y * _silu(gb)).astype(BF16)


def _prompt_proj_kernel(h_ref, w1_ref, w1t_ref, bfr_ref, bfc_ref, cw_ref, lvg_ref, lvb_ref, ws_ref, bst_ref,
                        wb_ref, wc_ref,
                        qa_ref, ka_ref, va_ref, ko_ref, vo_ref, lf_ref, tail_ref, sga_ref, sma_ref, mbc_ref,
                        crow_ref, ccol_ref, zbuf_ref, *, tm, d_model):
    off, _ = _offsets(d_model)
    i = pl.program_id(1)

    @pl.when(i == 0)
    def _():
        crow_ref[...] = jnp.zeros_like(crow_ref)
        ccol_ref[...] = jnp.zeros_like(ccol_ref)
        zbuf_ref[0:SUBLANES, :] = jnp.zeros((SUBLANES, BRANCH_W), F32)

    hb = h_ref[0].astype(BF16)

    def proj(name, width=BRANCH_W):
        return _dot(hb, w1_ref[:, off[name]:off[name] + width])

    pt = _dot_nt(w1t_ref[...], hb)

    ri = lax.broadcasted_iota(jnp.int32, (tm, tm), 0)
    ci = lax.broadcasted_iota(jnp.int32, (tm, tm), 1)
    logf_r = _log_sigmoid(proj("f", FORGET_PAD) + bfr_ref[...])
    c_r = crow_ref[0:1, :]
    for piece in _split3(logf_r):
        c_r = c_r + _dot(jnp.where(ri >= ci, 1.0, 0.0).astype(BF16), piece)
    crow_ref[0:1, :] = c_r[tm - 1:tm, :]
    logf_c = _log_sigmoid(pt[2 * BRANCH_W:2 * BRANCH_W + BF16_ROWS] + bfc_ref[:, 0:1])
    lf_ref[0] = logf_c[0:N_HEADS]
    c_c = ccol_ref[:, 0:1]
    for piece in _split3(logf_c):
        c_c = c_c + _dot(piece, jnp.where(ri <= ci, 1.0, 0.0).astype(BF16))
    ccol_ref[...] = jnp.broadcast_to(c_c[:, tm - 1:tm], ccol_ref.shape)
    c3r = [p.astype(F32) for p in _split3(c_r * LOG2E)]
    c3c = [p.astype(F32) for p in _split3(c_c * LOG2E)]

    k = proj("k")
    ko_ref[0] = k
    vo_ref[0] = pt[BRANCH_W:2 * BRANCH_W]
    kh = _head_tiles(k)
    lane = lax.broadcasted_iota(jnp.int32, (tm, LANES), 1)
    row8 = lax.broadcasted_iota(jnp.int32, (SUBLANES, tm), 0)
    zpad = jnp.zeros((LANES - HEAD_DIM - SUBLANES, tm), F32)
    ones_row = jnp.where(row8 == 0, 1.0, 0.0)
    for h in range(N_HEADS):
        cb3 = [jnp.broadcast_to(p[:, h:h + 1], (tm, LANES)) for p in c3r]
        ak = jnp.where(lane < AUG0 + 3, 1.0, jnp.where(lane == AUG0 + 3, -cb3[0], jnp.where(
            lane == AUG0 + 4, -cb3[1], jnp.where(lane == AUG0 + 5, -cb3[2], 0.0))))
        ka_ref[0, h] = jnp.where(lane < HEAD_DIM, kh[h], ak).astype(BF16)
        aq = jnp.where(row8 == 0, c3c[0][h:h + 1], jnp.where(row8 == 1, c3c[1][h:h + 1], jnp.where(
            row8 == 2, c3c[2][h:h + 1], jnp.where(row8 < 6, 1.0, 0.0))))
        qt = pt[h * HEAD_DIM:(h + 1) * HEAD_DIM] * (SCALE * LOG2E)
        qa_ref[0, h] = jnp.concatenate([qt, aq, zpad], axis=0).astype(BF16)
        vt = pt[BRANCH_W + h * HEAD_DIM:BRANCH_W + (h + 1) * HEAD_DIM]
        va_ref[0, h] = jnp.concatenate([vt, ones_row, zpad], axis=0).astype(BF16)

    sga_ref[0] = _silu(proj("ga")).astype(BF16)

    z = proj("cc") * proj("ch")
    zbuf_ref[SUBLANES:SUBLANES + tm, :] = z
    z1 = zbuf_ref[SUBLANES - 1:SUBLANES - 1 + tm, :]
    z2 = zbuf_ref[SUBLANES - 2:SUBLANES - 2 + tm, :]
    zbuf_ref[0:SUBLANES, :] = z[tm - SUBLANES:tm, :]
    tail_ref[0] = z[tm - (CONV_WIDTH - 1):tm, :]
    ob = _dot(_conv_gate(z, z1, z2, proj("cb"), proj("gb"), cw_ref), wb_ref[...])

    vn = _layer_norm(proj("vc"), lvg_ref[...], lvb_ref[...]).astype(BF16)
    tril = lax.broadcasted_iota(jnp.int32, (CHUNK, CHUNK), 0) >= lax.broadcasted_iota(jnp.int32, (CHUNK, CHUNK), 1)
    wm = [jnp.where(tril, ws_ref[g], 0.0).astype(BF16) for g in range(N_GROUPS)]
    rows = []
    for cidx in range(tm // CHUNK):
        rows.append(jnp.concatenate(
            [_dot(wm[g], vn[cidx * CHUNK:(cidx + 1) * CHUNK, g * GROUP_DIM:(g + 1) * GROUP_DIM]) + bst_ref[:, g:g + 1]
             for g in range(N_GROUPS)], axis=1))
    s = jnp.concatenate(rows, axis=0)
    oc = _dot((proj("u") * s * _silu(proj("gc"))).astype(BF16), wc_ref[...])

    sma_ref[0] = _sigmoid(proj("ma", d_model)).astype(BF16)
    mbc_ref[0] = (_sigmoid(proj("mb", d_model)) * ob + _sigmoid(proj("mc", d_model)) * oc).astype(BF16)


def _prompt_proj(h, w1, w1t, bfr, bfc, cw, lvg, lvb, ws, bst, wb, wc, tm):
    b, s, d = h.shape
    wtot = w1.shape[1]
    const = lambda shape: pl.BlockSpec(shape, lambda bi, i: (0,) * len(shape), pipeline_mode=pl.Buffered(1))
    rows = lambda w: pl.BlockSpec((1, tm, w), lambda bi, i: (bi, i, 0))
    cols = lambda r: pl.BlockSpec((1, r, tm), lambda bi, i: (bi, 0, i))
    heads_r = pl.BlockSpec((1, N_HEADS, tm, LANES), lambda bi, i: (bi, 0, i, 0))
    heads_c = pl.BlockSpec((1, N_HEADS, LANES, tm), lambda bi, i: (bi, 0, 0, i))
    return pl.pallas_call(
        functools.partial(_prompt_proj_kernel, tm=tm, d_model=d),
        grid=(b, s // tm),
        in_specs=[rows(d), const((d, wtot)), const((W1T_ROWS, d)), const((1, FORGET_PAD)),
                  const((BF16_ROWS, LANES)), const((CONV_WIDTH, BRANCH_W)),
                  const((1, BRANCH_W)), const((1, BRANCH_W)), const((N_GROUPS, CHUNK, CHUNK)),
                  const((CHUNK, N_GROUPS)), const((BRANCH_W, d)), const((BRANCH_W, d))],
        out_specs=[heads_c, heads_r, heads_c, rows(BRANCH_W), cols(BRANCH_W), cols(N_HEADS),
                   pl.BlockSpec((1, CONV_WIDTH - 1, BRANCH_W), lambda bi, i: (bi, 0, 0)),
                   rows(BRANCH_W), rows(d), rows(d)],
        out_shape=[jax.ShapeDtypeStruct((b, N_HEADS, LANES, s), BF16), jax.ShapeDtypeStruct((b, N_HEADS, s, LANES), BF16),
                   jax.ShapeDtypeStruct((b, N_HEADS, LANES, s), BF16),
                   jax.ShapeDtypeStruct((b, s, BRANCH_W), F32), jax.ShapeDtypeStruct((b, BRANCH_W, s), F32),
                   jax.ShapeDtypeStruct((b, N_HEADS, s), F32),
                   jax.ShapeDtypeStruct((b, CONV_WIDTH - 1, BRANCH_W), F32),
                   jax.ShapeDtypeStruct((b, s, BRANCH_W), BF16), jax.ShapeDtypeStruct((b, s, d), BF16),
                   jax.ShapeDtypeStruct((b, s, d), BF16)],
        scratch_shapes=[pltpu.VMEM((SUBLANES, LANES), F32), pltpu.VMEM((BF16_ROWS, LANES), F32),
                        pltpu.VMEM((tm + SUBLANES, BRANCH_W), F32)],
        compiler_params=pltpu.CompilerParams(dimension_semantics=("arbitrary", "arbitrary"),
                                             vmem_limit_bytes=VMEM_LIMIT),
        name="prompt_proj",
    )(h, w1, w1t, bfr, bfc, cw, lvg, lvb, ws, bst, wb, wc)


def _flash_kernel(qi_ref, ki_ref, qt_ref, k_ref, vt_ref, o_ref, m_ref, acc_ref, *, tq, tk):
    p = pl.program_id(1)
    qi = qi_ref[p]
    ki = ki_ref[p]

    @pl.when(ki == 0)
    def _():
        m_ref[...] = jnp.full_like(m_ref, -jnp.inf)
        acc_ref[...] = jnp.zeros_like(acc_ref)

    def body(masked):
        if masked:
            mask = (ki * tk + lax.broadcasted_iota(jnp.int32, (tk, tq), 0)
                    <= qi * tq + lax.broadcasted_iota(jnp.int32, (tk, tq), 1))

        def scores(h):
            s = _dot(k_ref[0, h], qt_ref[0, h])
            return jnp.where(mask, s, -jnp.inf) if masked else s

        def softmax(h, s):
            m_old = m_ref[h]
            m_new = jnp.maximum(m_old, jnp.max(s, axis=0, keepdims=True))
            m_ref[h] = m_new
            return jnp.exp2(s - m_new).astype(BF16), jnp.exp2(m_old - m_new)

        def accumulate(h, pexp, a):
            acc_ref[h] = a * acc_ref[h] + _dot(vt_ref[0, h], pexp)

        s_next = scores(0)
        prev = None
        for h in range(N_HEADS):
            s_cur = s_next
            if h + 1 < N_HEADS:
                s_next = scores(h + 1)
            cur = softmax(h, s_cur)
            if prev is not None:
                accumulate(h - 1, *prev)
            prev = cur
        accumulate(N_HEADS - 1, *prev)

    full = (ki + 1) * tk - 1 <= qi * tq

    @pl.when(full)
    def _():
        body(False)

    @pl.when(jnp.logical_not(full))
    def _():
        body(True)

    @pl.when(ki == (qi * tq + tq - 1) // tk)
    def _():
        for j in range(N_HEADS // 2):
            a0, a1 = acc_ref[2 * j], acc_ref[2 * j + 1]
            yy = jnp.concatenate([a0[:HEAD_DIM] * (1.0 / a0[AUG0:AUG0 + 1]),
                                  a1[:HEAD_DIM] * (1.0 / a1[AUG0:AUG0 + 1])], axis=0)
            o_ref[0, :, j * LANES:(j + 1) * LANES] = yy.T.astype(BF16)


def _flash(qa, ka, va, tq, tk):
    b, _, s, _ = ka.shape
    pairs = [(qi, ki) for qi in range(s // tq) for ki in range((qi * tq + tq - 1) // tk + 1)]
    qi_tab = jnp.asarray(np.array([p[0] for p in pairs], np.int32))
    ki_tab = jnp.asarray(np.array([p[1] for p in pairs], np.int32))
    grid_spec = pltpu.PrefetchScalarGridSpec(
        num_scalar_prefetch=2,
        grid=(b, len(pairs)),
        in_specs=[pl.BlockSpec((1, N_HEADS, LANES, tq), lambda bi, p, qt, kt: (bi, 0, 0, qt[p])),
                  pl.BlockSpec((1, N_HEADS, tk, LANES), lambda bi, p, qt, kt: (bi, 0, kt[p], 0)),
                  pl.BlockSpec((1, N_HEADS, LANES, tk), lambda bi, p, qt, kt: (bi, 0, 0, kt[p]))],
        out_specs=pl.BlockSpec((1, tq, BRANCH_W), lambda bi, p, qt, kt: (bi, qt[p], 0)),
        scratch_shapes=[pltpu.VMEM((N_HEADS, 1, tq), F32), pltpu.VMEM((N_HEADS, LANES, tq), F32)],
    )
    return pl.pallas_call(
        functools.partial(_flash_kernel, tq=tq, tk=tk),
        grid_spec=grid_spec,
        out_shape=jax.ShapeDtypeStruct((b, s, BRANCH_W), BF16),
        compiler_params=pltpu.CompilerParams(dimension_semantics=("arbitrary", "arbitrary"),
                                             vmem_limit_bytes=VMEM_LIMIT),
        name="prompt_flash",
    )(qi_tab, ki_tab, qa, ka, va)


def _merge_kernel(h_ref, ya_ref, sga_ref, sma_ref, mbc_ref, wa_ref, wo_ref, g_ref, b_ref, o_ref, *, alpha):
    oa = _dot((ya_ref[...].astype(F32) * sga_ref[...].astype(F32)).astype(BF16), wa_ref[...])
    m = sma_ref[...].astype(F32) * oa + mbc_ref[...].astype(F32)
    x = alpha * h_ref[...] + _dot(m.astype(BF16), wo_ref[...])
    o_ref[...] = _layer_norm(x, g_ref[...], b_ref[...])


def _merge(h, ya, sga, sma, mbc, wa, wo, g, b, alpha, tm):
    n, d = h.shape
    rows = lambda w: pl.BlockSpec((tm, w), lambda i: (i, 0))
    const = lambda shape: pl.BlockSpec(shape, lambda i: (0,) * len(shape))
    return pl.pallas_call(
        functools.partial(_merge_kernel, alpha=alpha),
        grid=(n // tm,),
        in_specs=[rows(d), rows(BRANCH_W), rows(BRANCH_W), rows(d), rows(d),
                  const((BRANCH_W, d)), const((d, d)), const((1, d)), const((1, d))],
        out_specs=rows(d),
        out_shape=jax.ShapeDtypeStruct((n, d), F32),
        compiler_params=pltpu.CompilerParams(dimension_semantics=("arbitrary",), vmem_limit_bytes=VMEM_LIMIT),
        name="merge",
    )(h, ya, sga, sma, mbc, wa, wo, g, b)


def _sample_proj_kernel(h_ref, w1_ref, w1t_ref, bf_ref, cw_ref, lvg_ref, lvb_ref, wst_ref, bsr_ref, wb_ref, wc_ref,
                        zp1_ref, zp2_ref,
                        q_ref, k_ref, v_ref, lf_ref, cnt_ref, z_ref, vn_ref, sga_ref, sma_ref, mbc_ref,
                        p_ref, zbuf_ref, *, tn, n_steps, t_new, d_model):
    off, _ = _offsets(d_model)
    j = pl.program_id(0)
    ns = h_ref.shape[0]
    hb = h_ref[...].astype(BF16)
    col = pl.multiple_of(j * tn, LANES)
    p_ref[:, pl.ds(col, tn)] = _dot(hb, w1_ref[...])

    @pl.when(j == n_steps - 1)
    def _():
        def proj(name, width=BRANCH_W):
            return p_ref[:, off[name]:off[name] + width]

        qv = _dot_nt(hb, w1t_ref[...])
        q_ref[...] = qv[:, 0:BRANCH_W]
        v_ref[...] = qv[:, BRANCH_W:2 * BRANCH_W]
        k_ref[...] = proj("k")
        logf = _log_sigmoid(proj("f", FORGET_PAD) + bf_ref[...])
        lf_ref[...] = logf[:, :N_HEADS]
        lt = logf.T
        tpos = lax.broadcasted_iota(jnp.int32, lt.shape, 1) % t_new
        cn = lt
        shift = 1
        while shift < t_new:
            cn = cn + jnp.where(tpos >= shift, pltpu.roll(cn, shift, axis=1), 0.0)
            shift *= 2
        cnt_ref[...] = cn[0:N_HEADS, :]

        sga_ref[...] = _silu(proj("ga")).astype(BF16)

        z = proj("cc") * proj("ch")
        z_ref[...] = z
        zbuf_ref[0:SUBLANES, :] = jnp.zeros((SUBLANES, BRANCH_W), F32)
        zbuf_ref[SUBLANES:SUBLANES + ns, :] = z
        trow = lax.broadcasted_iota(jnp.int32, (ns, BRANCH_W), 0) % t_new
        z1 = jnp.where(trow < 1, zp1_ref[...], zbuf_ref[SUBLANES - 1:SUBLANES - 1 + ns, :])
        z2 = jnp.where(trow < 2, zp2_ref[...], zbuf_ref[SUBLANES - 2:SUBLANES - 2 + ns, :])
        ob = _dot(_conv_gate(z, z1, z2, proj("cb"), proj("gb"), cw_ref), wb_ref[...])

        vn = _layer_norm(proj("vc"), lvg_ref[...], lvb_ref[...])
        vn_ref[...] = vn
        vnb = vn.astype(BF16)
        r = lax.broadcasted_iota(jnp.int32, (ns, ns), 0)
        c = lax.broadcasted_iota(jnp.int32, (ns, ns), 1)
        keep = (r // t_new == c // t_new) & (c % t_new <= r % t_new)
        s = jnp.concatenate(
            [_dot(jnp.where(keep, wst_ref[g], 0.0).astype(BF16), vnb[:, g * GROUP_DIM:(g + 1) * GROUP_DIM])
             + bsr_ref[:, g:g + 1] for g in range(N_GROUPS)], axis=1)
        oc = _dot((proj("u") * s * _silu(proj("gc"))).astype(BF16), wc_ref[...])

        sma_ref[...] = _sigmoid(proj("ma", d_model)).astype(BF16)
        mbc_ref[...] = (_sigmoid(proj("mb", d_model)) * ob + _sigmoid(proj("mc", d_model)) * oc).astype(BF16)


def _sample_proj(h, w1, w1t, bf, cw, lvg, lvb, wst, bsr, wb, wc, zp1, zp2, t_new):
    ns, d = h.shape
    wtot = w1.shape[1]
    tn = W1_ALIGN
    n_steps = wtot // tn
    const = lambda shape: pl.BlockSpec(shape, lambda j: (0,) * len(shape))
    f32 = lambda w: jax.ShapeDtypeStruct((ns, w), F32)
    bf16 = lambda w: jax.ShapeDtypeStruct((ns, w), BF16)
    return pl.pallas_call(
        functools.partial(_sample_proj_kernel, tn=tn, n_steps=n_steps, t_new=t_new, d_model=d),
        grid=(n_steps,),
        in_specs=[const((ns, d)), pl.BlockSpec((d, tn), lambda j: (0, j)), const((W1T_ROWS, d)),
                  const((1, FORGET_PAD)),
                  const((CONV_WIDTH, BRANCH_W)), const((1, BRANCH_W)), const((1, BRANCH_W)),
                  const((N_GROUPS, ns, ns)), const((ns, N_GROUPS)), const((BRANCH_W, d)), const((BRANCH_W, d)),
                  const((ns, BRANCH_W)), const((ns, BRANCH_W))],
        out_specs=[const((ns, BRANCH_W)), const((ns, BRANCH_W)), const((ns, BRANCH_W)), const((ns, N_HEADS)),
                   const((N_HEADS, ns)), const((ns, BRANCH_W)), const((ns, BRANCH_W)),
                   const((ns, BRANCH_W)), const((ns, d)), const((ns, d))],
        out_shape=[f32(BRANCH_W), f32(BRANCH_W), f32(BRANCH_W), f32(N_HEADS),
                   jax.ShapeDtypeStruct((N_HEADS, ns), F32), f32(BRANCH_W), f32(BRANCH_W),
                   bf16(BRANCH_W), bf16(d), bf16(d)],
        scratch_shapes=[pltpu.VMEM((ns, wtot), F32), pltpu.VMEM((ns + SUBLANES, BRANCH_W), F32)],
        compiler_params=pltpu.CompilerParams(dimension_semantics=("arbitrary",), vmem_limit_bytes=VMEM_LIMIT),
        name="sample_proj",
    )(h, w1, w1t, bf, cw, lvg, lvb, wst, bsr, wb, wc, zp1, zp2)


def _sample_attn_kernel(layer_ref, pt_ref, q_ref, kn_ref, vn_ref, cn_ref, *rest, pp, t_new):
    del layer_ref, pt_ref
    k_refs, v_refs, lf_refs = rest[0:pp], rest[pp:2 * pp], rest[2 * pp:3 * pp]
    o_ref, m_ref, l_ref, acc_ref, car_ref = rest[3 * pp:]
    step = pl.program_id(1)
    n_steps = pl.num_programs(1)
    tpad = q_ref.shape[2]

    @pl.when(step == 0)
    def _():
        car_ref[...] = jnp.zeros_like(car_ref)
        q3 = q_ref[0] * SCALE
        trow = lax.broadcasted_iota(jnp.int32, (N_HEADS, tpad, 1), 1)
        ss = []
        for t in range(t_new):
            st = jnp.sum(q3 * kn_ref[0, :, t:t + 1, :], axis=2, keepdims=True) - cn_ref[0, :, t:t + 1][:, :, None]
            ss.append(jnp.where(trow >= t, st, -jnp.inf))
        m = ss[0]
        for st in ss[1:]:
            m = jnp.maximum(m, st)
        l = jnp.zeros((N_HEADS, tpad, 1), F32)
        acc = jnp.zeros((N_HEADS, tpad, HEAD_DIM), F32)
        for t in range(t_new):
            pt = jnp.exp(ss[t] - m)
            l = l + pt
            acc = acc + pt * vn_ref[0, :, t:t + 1, :]
        m_ref[...] = m
        l_ref[...] = l
        acc_ref[...] = acc

    ci = lax.broadcasted_iota(jnp.int32, (LANES, 2 * LANES), 0)
    cj = lax.broadcasted_iota(jnp.int32, (LANES, 2 * LANES), 1)
    usum = jnp.where((cj >= LANES) | (ci > cj), 1.0, 0.0).astype(BF16)
    carry = car_ref[...]
    bias = []
    for i in range(pp):
        r = jnp.zeros((N_HEADS, 2 * LANES), F32)
        for piece in _split3(lf_refs[i][0, 0]):
            r = r + _dot(piece, usum)
        bias.append(r[:, :LANES] + carry)
        carry = carry + r[:, LANES:]
    car_ref[...] = carry
    bias = jnp.concatenate(bias, axis=1)

    ss = []
    for h in range(N_HEADS):
        qh = (q_ref[0, h] * SCALE).astype(BF16)
        kt = jnp.concatenate([k_refs[i][0, 0, h] for i in range(pp)], axis=1).astype(BF16)
        ss.append(_dot(qh, kt) + bias[h:h + 1, :])
    ps, al = [], []
    for h in range(N_HEADS):
        m_old = m_ref[h]
        m_new = jnp.maximum(m_old, jnp.max(ss[h], axis=1, keepdims=True))
        a = jnp.exp(m_old - m_new)
        p = jnp.exp(ss[h] - m_new)
        l_ref[h] = a * l_ref[h] + jnp.sum(p, axis=1, keepdims=True)
        m_ref[h] = m_new
        ps.append(p.astype(BF16))
        al.append(a)
    for h in range(N_HEADS):
        vt = jnp.concatenate([v_refs[i][0, 0, h] for i in range(pp)], axis=1).astype(BF16)
        acc_ref[h] = al[h] * acc_ref[h] + _dot_nt(ps[h], vt)

    @pl.when(step == n_steps - 1)
    def _():
        for h in range(N_HEADS):
            o_ref[0, h] = acc_ref[h] * (1.0 / l_ref[h])


def _sample_attn(layer, page_table, q, kn, vn, cn, cache_kt, cache_vt, cache_lft, t_new):
    nb, tpad = q.shape[0], q.shape[2]
    n_pages = page_table.shape[1]
    page = cache_kt.shape[4]
    pp = PAGES_PER_STEP if n_pages % PAGES_PER_STEP == 0 else n_pages
    n_steps = n_pages // pp

    def page_map(i):
        def index_map(b, s, lref, ptref):
            return (lref[0], ptref[b * n_pages + (n_pages - 1 - (s * pp + i))], 0, 0, 0)
        return index_map

    def lf_map(i):
        def index_map(b, s, lref, ptref):
            return (lref[0], ptref[b * n_pages + (n_pages - 1 - (s * pp + i))], 0, 0)
        return index_map

    per_seq = lambda shape: pl.BlockSpec((1,) + shape, lambda b, s, lref, ptref: (b,) + (0,) * len(shape))
    in_specs = [per_seq((N_HEADS, tpad, HEAD_DIM))] * 3 + [per_seq((N_HEADS, LANES))]
    in_specs += [pl.BlockSpec((1, 1, N_HEADS, HEAD_DIM, page), page_map(i)) for i in range(pp)] * 2
    in_specs += [pl.BlockSpec((1, 1, N_HEADS, page), lf_map(i)) for i in range(pp)]
    grid_spec = pltpu.PrefetchScalarGridSpec(
        num_scalar_prefetch=2,
        grid=(nb, n_steps),
        in_specs=in_specs,
        out_specs=per_seq((N_HEADS, tpad, HEAD_DIM)),
        scratch_shapes=[pltpu.VMEM((N_HEADS, tpad, 1), F32), pltpu.VMEM((N_HEADS, tpad, 1), F32),
                        pltpu.VMEM((N_HEADS, tpad, HEAD_DIM), F32), pltpu.VMEM((N_HEADS, LANES), F32)],
    )
    return pl.pallas_call(
        functools.partial(_sample_attn_kernel, pp=pp, t_new=t_new),
        grid_spec=grid_spec,
        out_shape=jax.ShapeDtypeStruct((nb, N_HEADS, tpad, HEAD_DIM), F32),
        compiler_params=pltpu.CompilerParams(dimension_semantics=("arbitrary", "arbitrary"),
                                             vmem_limit_bytes=VMEM_LIMIT),
        name="sample_attn",
    )(layer, page_table.reshape(-1), q, kn, vn, cn,
      *([cache_kt] * pp), *([cache_vt] * pp), *([cache_lft] * pp))


def _pick_tile(n, pref):
    t = pref
    while n % t:
        t //= 2
    return t


def kernel(x_prompt, x_sample, cache_k, cache_v, cache_logf, state_conv, page_table, ln_in_g, ln_in_b, w_in, b_f,
           conv_w, ln_v_g, ln_v_b, w_s, b_s, w_a_out, w_b_out, w_c_out, w_o, ln_g, ln_b):
    depth, d = w_in.shape[0], w_in.shape[1]
    bsz, seq, _ = x_prompt.shape
    nb, t_new, _ = x_sample.shape
    ns = nb * t_new
    assert seq % CHUNK == 0 and t_new <= SUBLANES and page_table.shape[0] == nb
    alpha = (2.0 * depth) ** 0.25
    tm = _pick_tile(seq, 512)
    tq = _pick_tile(seq, 512)

    _, wtot = _offsets(d)
    q0, k0, v0, f0, g0 = 0, BRANCH_W, 2 * BRANCH_W, 3 * BRANCH_W, 3 * BRANCH_W + N_HEADS
    w_f = w_in[..., f0:g0]
    used = w_in.shape[2] - 2 * BRANCH_W - N_HEADS + FORGET_PAD
    w1 = jnp.concatenate([w_in[..., k0:v0], w_in[..., g0:],
                          jnp.pad(w_f, ((0, 0), (0, 0), (0, wtot - used + FORGET_PAD - N_HEADS)))],
                         axis=-1).astype(BF16)
    w1t = jnp.swapaxes(jnp.concatenate([w_in[..., q0:k0], w_in[..., v0:f0],
                                        jnp.pad(w_f, ((0, 0), (0, 0), (0, FORGET_PAD - N_HEADS)))], axis=-1),
                       1, 2).astype(BF16)
    b_f_pad = jnp.pad(b_f, ((0, 0), (0, FORGET_PAD - N_HEADS)))
    bfr = b_f_pad.reshape(depth, 1, FORGET_PAD)
    bfc = jnp.broadcast_to(b_f_pad[:, :BF16_ROWS, None], (depth, BF16_ROWS, LANES))
    wa, wb, wc, wo = (w.astype(BF16) for w in (w_a_out, w_b_out, w_c_out, w_o))
    bst = jnp.swapaxes(b_s, 1, 2)
    wst = jnp.tile(w_s[:, :, :t_new, :t_new], (1, 1, ns // t_new, ns // t_new))
    bsr = jnp.tile(jnp.swapaxes(b_s[:, :, :t_new], 1, 2), (1, ns // t_new, 1))
    cache_kt = jnp.transpose(cache_k, (0, 1, 3, 4, 2))
    cache_vt = jnp.transpose(cache_v, (0, 1, 3, 4, 2))
    cache_lft = jnp.swapaxes(cache_logf, 2, 3)
    zp1 = jnp.concatenate([state_conv[:, :, 1:2], jnp.zeros((depth, nb, t_new - 1, BRANCH_W), F32)], axis=2)
    zp2 = jnp.concatenate([state_conv, jnp.zeros((depth, nb, t_new - 2, BRANCH_W), F32)], axis=2)
    zp1, zp2 = zp1.reshape(depth, ns, BRANCH_W), zp2.reshape(depth, ns, BRANCH_W)

    hp = _ln_rows(x_prompt.reshape(bsz * seq, d), ln_in_g, ln_in_b, tm).reshape(bsz, seq, d)
    hs = _ln_rows(x_sample.reshape(ns, d), ln_in_g, ln_in_b, ns)

    outs = {n: [] for n in ("kp", "vp", "lp", "cp", "ks", "vs", "ls", "cs", "us")}
    for l in range(depth):
        row1 = lambda a: a[l].reshape(1, -1)
        qa, ka, va, ko, vo, lf, tail, sga, sma, mbc = _prompt_proj(
            hp, w1[l], w1t[l], bfr[l], bfc[l], conv_w[l], row1(ln_v_g), row1(ln_v_b), w_s[l], bst[l], wb[l], wc[l], tm)
        ya = _flash(qa, ka, va, tq, tq)
        flat = lambda a: a.reshape(bsz * seq, a.shape[-1])
        hp = _merge(flat(hp), flat(ya), flat(sga), flat(sma), flat(mbc), wa[l], wo[l], row1(ln_g), row1(ln_b),
                    alpha, tm).reshape(bsz, seq, d)
        outs["kp"].append(ko); outs["vp"].append(vo); outs["lp"].append(lf); outs["cp"].append(tail)
        qs, ks, vs, lfs, cnt, zs, vns, sga, sma, mbc = _sample_proj(
            hs, w1[l], w1t[l], bfr[l], conv_w[l], row1(ln_v_g), row1(ln_v_b), wst[l], bsr[l], wb[l], wc[l],
            zp1[l], zp2[l], t_new)
        head_major = lambda a: jnp.pad(jnp.transpose(a.reshape(nb, t_new, N_HEADS, HEAD_DIM), (0, 2, 1, 3)),
                                       ((0, 0), (0, 0), (0, SUBLANES - t_new), (0, 0)))
        cn = jnp.pad(jnp.transpose(cnt.reshape(N_HEADS, nb, t_new), (1, 0, 2)), ((0, 0), (0, 0), (0, LANES - t_new)))
        yas = _sample_attn(jnp.full((1,), l, jnp.int32), page_table, head_major(qs), head_major(ks), head_major(vs),
                           cn, cache_kt, cache_vt, cache_lft, t_new)
        yas = jnp.transpose(yas[:, :, :t_new], (0, 2, 1, 3)).reshape(ns, BRANCH_W)
        hs = _merge(hs, yas.astype(BF16), sga, sma, mbc, wa[l], wo[l], row1(ln_g), row1(ln_b), alpha, ns)
        outs["ks"].append(ks); outs["vs"].append(vs); outs["ls"].append(lfs)
        outs["cs"].append(zs.reshape(nb, t_new, BRANCH_W)[:, t_new - (CONV_WIDTH - 1):])
        outs["us"].append(vns)

    st = {n: jnp.stack(v) for n, v in outs.items()}
    new_v_prompt = jnp.transpose(st["vp"].reshape(depth, bsz, N_HEADS, HEAD_DIM, seq), (0, 1, 4, 2, 3))
    return (hp, hs.reshape(nb, t_new, d),
            st["kp"].reshape(depth, bsz, seq, N_HEADS, HEAD_DIM), new_v_prompt,
            jnp.swapaxes(st["lp"], 2, 3), st["cp"],
            st["ks"].reshape(depth, nb, t_new, N_HEADS, HEAD_DIM), st["vs"].reshape(depth, nb, t_new, N_HEADS, HEAD_DIM),
            st["ls"].reshape(depth, nb, t_new, N_HEADS), st["cs"], st["us"].reshape(depth, nb, t_new, BRANCH_W))
```

```python
import functools
import math

import numpy as np
import jax
import jax.numpy as jnp
from jax import lax
from jax.experimental import pallas as pl
from jax.experimental.pallas import tpu as pltpu

F32 = jnp.float32
BF16 = jnp.bfloat16

N_HEADS = 8
HEAD_DIM = 64
BRANCH_W = 512
CHUNK = 128
N_GROUPS = 4
GROUP_DIM = BRANCH_W // N_GROUPS
CONV_WIDTH = 3
LN_EPS = 1e-5
LANES = 128
SUBLANES = 8
BF16_ROWS = 16
FORGET_PAD = LANES
ROW_SLOTS = ("k", "ga", "cb", "cc", "ch", "gb", "u", "vc", "gc")
W1_ALIGN = 2048
W1T_ROWS = 2 * BRANCH_W + FORGET_PAD
SCALE = HEAD_DIM ** -0.5
LOG2E = math.log2(math.e)
VMEM_LIMIT = 56 * 1024 * 1024
PAGES_PER_STEP = 16

AUG0 = HEAD_DIM


def _offsets(d_model):
    off = {n: i * BRANCH_W for i, n in enumerate(ROW_SLOTS)}
    base = len(ROW_SLOTS) * BRANCH_W
    off["ma"], off["mb"], off["mc"] = base, base + d_model, base + 2 * d_model
    used = base + 3 * d_model
    return off, -(-used // W1_ALIGN) * W1_ALIGN


def _dot(a, b):
    return jnp.dot(a, b, preferred_element_type=F32)


def _dot_nt(a, b):
    return lax.dot_general(a, b, (((1,), (1,)), ((), ())), preferred_element_type=F32)


def _split3(x):
    hi = x.astype(BF16)
    r = x - hi.astype(F32)
    mid = r.astype(BF16)
    lo = (r - mid.astype(F32)).astype(BF16)
    return hi, mid, lo


def _sigmoid(x):
    return 1.0 / (1.0 + jnp.exp(-x))


def _silu(x):
    return x * _sigmoid(x)


def _log_sigmoid(x):
    return jnp.minimum(x, 0.0) - jnp.log1p(jnp.exp(-jnp.abs(x)))


def _layer_norm(x, g, b):
    mu = jnp.mean(x, axis=-1, keepdims=True)
    xc = x - mu
    var = jnp.mean(xc * xc, axis=-1, keepdims=True)
    return xc * lax.rsqrt(var + LN_EPS) * g + b


def _head_tiles(x):
    out = []
    for j in range(BRANCH_W // LANES):
        blk = x[:, j * LANES:(j + 1) * LANES]
        out.append(blk)
        out.append(pltpu.roll(blk, HEAD_DIM, axis=1))
    return out


def _ln_kernel(x_ref, g_ref, b_ref, o_ref):
    o_ref[...] = _layer_norm(x_ref[...], g_ref[...], b_ref[...])


def _ln_rows(x, g, b, tm):
    n, d = x.shape
    return pl.pallas_call(
        _ln_kernel,
        grid=(n // tm,),
        in_specs=[pl.BlockSpec((tm, d), lambda i: (i, 0)),
                  pl.BlockSpec((1, d), lambda i: (0, 0)),
                  pl.BlockSpec((1, d), lambda i: (0, 0))],
        out_specs=pl.BlockSpec((tm, d), lambda i: (i, 0)),
        out_shape=jax.ShapeDtypeStruct((n, d), F32),
        name="ln_in",
    )(x, g.reshape(1, d), b.reshape(1, d))


def _conv_gate(z, z1, z2, cb, gb, cw_ref):
    y = cw_ref[0:1, :] * z2 + cw_ref[1:2, :] * z1 + cw_ref[2:3, :] * z
    return (cb * y * _silu(gb)).astype(BF16)


def _prompt_proj_kernel(h_ref, w1_ref, w1t_ref, bfc_ref, cw_ref, lvg_ref, lvb_ref, ws_ref, bst_ref,
                        wb_ref, wc_ref,
                        qa_ref, ka_ref, va_ref, ko_ref, vo_ref, lf_ref, tail_ref, sga_ref, sma_ref, mbc_ref,
                        ccol_ref, zbuf_ref, *, tm, d_model):
    off, _ = _offsets(d_model)
    i = pl.program_id(1)

    @pl.when(i == 0)
    def _():
        ccol_ref[...] = jnp.zeros_like(ccol_ref)
        zbuf_ref[0:SUBLANES, :] = jnp.zeros((SUBLANES, BRANCH_W), F32)

    hb = h_ref[0].astype(BF16)

    def proj(name, width=BRANCH_W):
        return _dot(hb, w1_ref[:, off[name]:off[name] + width])

    pt = _dot_nt(w1t_ref[0:2 * BRANCH_W + BF16_ROWS, :], hb)

    logf_c = _log_sigmoid(pt[2 * BRANCH_W:2 * BRANCH_W + BF16_ROWS] + bfc_ref[:, 0:1])
    lf_ref[0] = logf_c[0:N_HEADS]
    upper = jnp.where(lax.broadcasted_iota(jnp.int32, (tm, tm), 0) <= lax.broadcasted_iota(jnp.int32, (tm, tm), 1),
                      1.0, 0.0).astype(BF16)
    c_c = ccol_ref[:, 0:1]
    for piece in _split3(logf_c):
        c_c = c_c + _dot(piece, upper)
    ccol_ref[...] = jnp.broadcast_to(c_c[:, tm - 1:tm], ccol_ref.shape)
    c_c = c_c * LOG2E
    c3c = [p.astype(F32) for p in _split3(c_c)]
    c_r = jnp.concatenate([c_c, jnp.zeros((LANES - BF16_ROWS, tm), F32)], axis=0).T
    c3r = [p.astype(F32) for p in _split3(c_r)]

    k = proj("k")
    ko_ref[0] = k
    vo_ref[0] = pt[BRANCH_W:2 * BRANCH_W]
    kh = _head_tiles(k)
    lane = lax.broadcasted_iota(jnp.int32, (tm, LANES), 1)
    row8 = lax.broadcasted_iota(jnp.int32, (SUBLANES, tm), 0)
    zpad = jnp.zeros((LANES - HEAD_DIM - SUBLANES, tm), F32)
    ones_row = jnp.where(row8 == 0, 1.0, 0.0)
    for h in range(N_HEADS):
        cb3 = [jnp.broadcast_to(p[:, h:h + 1], (tm, LANES)) for p in c3r]
        ak = jnp.where(lane < AUG0 + 3, 1.0, jnp.where(lane == AUG0 + 3, -cb3[0], jnp.where(
            lane == AUG0 + 4, -cb3[1], jnp.where(lane == AUG0 + 5, -cb3[2], 0.0))))
        ka_ref[0, h] = jnp.where(lane < HEAD_DIM, kh[h], ak).astype(BF16)
        aq = jnp.where(row8 == 0, c3c[0][h:h + 1], jnp.where(row8 == 1, c3c[1][h:h + 1], jnp.where(
            row8 == 2, c3c[2][h:h + 1], jnp.where(row8 < 6, 1.0, 0.0))))
        qt = pt[h * HEAD_DIM:(h + 1) * HEAD_DIM] * (SCALE * LOG2E)
        qa_ref[0, h] = jnp.concatenate([qt, aq, zpad], axis=0).astype(BF16)
        vt = pt[BRANCH_W + h * HEAD_DIM:BRANCH_W + (h + 1) * HEAD_DIM]
        va_ref[0, h] = jnp.concatenate([vt, ones_row, zpad], axis=0).astype(BF16)

    sga_ref[0] = _silu(proj("ga")).astype(BF16)

    z = proj("cc") * proj("ch")
    zbuf_ref[SUBLANES:SUBLANES + tm, :] = z
    z1 = zbuf_ref[SUBLANES - 1:SUBLANES - 1 + tm, :]
    z2 = zbuf_ref[SUBLANES - 2:SUBLANES - 2 + tm, :]
    zbuf_ref[0:SUBLANES, :] = z[tm - SUBLANES:tm, :]
    tail_ref[0] = z[tm - (CONV_WIDTH - 1):tm, :]
    ob = _dot(_conv_gate(z, z1, z2, proj("cb"), proj("gb"), cw_ref), wb_ref[...])

    vn = _layer_norm(proj("vc"), lvg_ref[...], lvb_ref[...]).astype(BF16)
    tril = lax.broadcasted_iota(jnp.int32, (CHUNK, CHUNK), 0) >= lax.broadcasted_iota(jnp.int32, (CHUNK, CHUNK), 1)
    wm = [jnp.where(tril, ws_ref[g], 0.0).astype(BF16) for g in range(N_GROUPS)]
    rows = []
    for cidx in range(tm // CHUNK):
        rows.append(jnp.concatenate(
            [_dot(wm[g], vn[cidx * CHUNK:(cidx + 1) * CHUNK, g * GROUP_DIM:(g + 1) * GROUP_DIM]) + bst_ref[:, g:g + 1]
             for g in range(N_GROUPS)], axis=1))
    s = jnp.concatenate(rows, axis=0)
    oc = _dot((proj("u") * s * _silu(proj("gc"))).astype(BF16), wc_ref[...])

    sma_ref[0] = _sigmoid(proj("ma", d_model)).astype(BF16)
    mbc_ref[0] = (_sigmoid(proj("mb", d_model)) * ob + _sigmoid(proj("mc", d_model)) * oc).astype(BF16)


def _prompt_proj(h, w1, w1t, bfc, cw, lvg, lvb, ws, bst, wb, wc, tm):
    b, s, d = h.shape
    wtot = w1.shape[1]
    const = lambda shape: pl.BlockSpec(shape, lambda bi, i: (0,) * len(shape), pipeline_mode=pl.Buffered(1))
    rows = lambda w: pl.BlockSpec((1, tm, w), lambda bi, i: (bi, i, 0))
    cols = lambda r: pl.BlockSpec((1, r, tm), lambda bi, i: (bi, 0, i))
    heads_r = pl.BlockSpec((1, N_HEADS, tm, LANES), lambda bi, i: (bi, 0, i, 0))
    heads_c = pl.BlockSpec((1, N_HEADS, LANES, tm), lambda bi, i: (bi, 0, 0, i))
    return pl.pallas_call(
        functools.partial(_prompt_proj_kernel, tm=tm, d_model=d),
        grid=(b, s // tm),
        in_specs=[rows(d), const((d, wtot)), const((W1T_ROWS, d)),
                  const((BF16_ROWS, LANES)), const((CONV_WIDTH, BRANCH_W)),
                  const((1, BRANCH_W)), const((1, BRANCH_W)), const((N_GROUPS, CHUNK, CHUNK)),
                  const((CHUNK, N_GROUPS)), const((BRANCH_W, d)), const((BRANCH_W, d))],
        out_specs=[heads_c, heads_r, heads_c, rows(BRANCH_W), cols(BRANCH_W), cols(N_HEADS),
                   pl.BlockSpec((1, CONV_WIDTH - 1, BRANCH_W), lambda bi, i: (bi, 0, 0)),
                   rows(BRANCH_W), rows(d), rows(d)],
        out_shape=[jax.ShapeDtypeStruct((b, N_HEADS, LANES, s), BF16), jax.ShapeDtypeStruct((b, N_HEADS, s, LANES), BF16),
                   jax.ShapeDtypeStruct((b, N_HEADS, LANES, s), BF16),
                   jax.ShapeDtypeStruct((b, s, BRANCH_W), F32), jax.ShapeDtypeStruct((b, BRANCH_W, s), F32),
                   jax.ShapeDtypeStruct((b, N_HEADS, s), F32),
                   jax.ShapeDtypeStruct((b, CONV_WIDTH - 1, BRANCH_W), F32),
                   jax.ShapeDtypeStruct((b, s, BRANCH_W), BF16), jax.ShapeDtypeStruct((b, s, d), BF16),
                   jax.ShapeDtypeStruct((b, s, d), BF16)],
        scratch_shapes=[pltpu.VMEM((BF16_ROWS, LANES), F32), pltpu.VMEM((tm + SUBLANES, BRANCH_W), F32)],
        compiler_params=pltpu.CompilerParams(dimension_semantics=("arbitrary", "arbitrary"),
                                             vmem_limit_bytes=VMEM_LIMIT),
        name="prompt_proj",
    )(h, w1, w1t, bfc, cw, lvg, lvb, ws, bst, wb, wc)


def _flash_kernel(qi_ref, ki_ref, qt_ref, k_ref, vt_ref, o_ref, m_ref, acc_ref, *, tq, tk, bsz):
    p = pl.program_id(0)
    qi = qi_ref[p]
    ki = ki_ref[p]
    n_units = bsz * N_HEADS

    @pl.when(ki == 0)
    def _():
        m_ref[...] = jnp.full_like(m_ref, -jnp.inf)
        acc_ref[...] = jnp.zeros_like(acc_ref)

    def body(masked):
        if masked:
            mask = (ki * tk + lax.broadcasted_iota(jnp.int32, (tk, tq), 0)
                    <= qi * tq + lax.broadcasted_iota(jnp.int32, (tk, tq), 1))

        def scores(u):
            b, h = divmod(u, N_HEADS)
            s = _dot(k_ref[b, h], qt_ref[b, h])
            return jnp.where(mask, s, -jnp.inf) if masked else s

        def softmax(u, s):
            m_old = m_ref[u]
            m_new = jnp.maximum(m_old, jnp.max(s, axis=0, keepdims=True))
            m_ref[u] = m_new
            return jnp.exp2(s - m_new).astype(BF16), jnp.exp2(m_old - m_new)

        def accumulate(u, pexp, a):
            b, h = divmod(u, N_HEADS)
            acc_ref[u] = a * acc_ref[u] + _dot(vt_ref[b, h], pexp)

        s_next = scores(0)
        prev = None
        for u in range(n_units):
            s_cur = s_next
            if u + 1 < n_units:
                s_next = scores(u + 1)
            cur = softmax(u, s_cur)
            if prev is not None:
                accumulate(u - 1, *prev)
            prev = cur
        accumulate(n_units - 1, *prev)

    full = (ki + 1) * tk - 1 <= qi * tq

    @pl.when(full)
    def _():
        body(False)

    @pl.when(jnp.logical_not(full))
    def _():
        body(True)

    @pl.when(ki == (qi * tq + tq - 1) // tk)
    def _():
        for b in range(bsz):
            for j in range(N_HEADS // 2):
                a0, a1 = acc_ref[b * N_HEADS + 2 * j], acc_ref[b * N_HEADS + 2 * j + 1]
                yy = jnp.concatenate([a0[:HEAD_DIM] * (1.0 / a0[AUG0:AUG0 + 1]),
                                      a1[:HEAD_DIM] * (1.0 / a1[AUG0:AUG0 + 1])], axis=0)
                o_ref[b, :, j * LANES:(j + 1) * LANES] = yy.T.astype(BF16)


def _flash(qa, ka, va, tq, tk):
    b, _, s, _ = ka.shape
    pairs = [(qi, ki) for qi in range(s // tq) for ki in range((qi * tq + tq - 1) // tk + 1)]
    qi_tab = jnp.asarray(np.array([p[0] for p in pairs], np.int32))
    ki_tab = jnp.asarray(np.array([p[1] for p in pairs], np.int32))
    grid_spec = pltpu.PrefetchScalarGridSpec(
        num_scalar_prefetch=2,
        grid=(len(pairs),),
        in_specs=[pl.BlockSpec((b, N_HEADS, LANES, tq), lambda p, qt, kt: (0, 0, 0, qt[p])),
                  pl.BlockSpec((b, N_HEADS, tk, LANES), lambda p, qt, kt: (0, 0, kt[p], 0)),
                  pl.BlockSpec((b, N_HEADS, LANES, tk), lambda p, qt, kt: (0, 0, 0, kt[p]))],
        out_specs=pl.BlockSpec((b, tq, BRANCH_W), lambda p, qt, kt: (0, qt[p], 0)),
        scratch_shapes=[pltpu.VMEM((b * N_HEADS, 1, tq), F32), pltpu.VMEM((b * N_HEADS, LANES, tq), F32)],
    )
    return pl.pallas_call(
        functools.partial(_flash_kernel, tq=tq, tk=tk, bsz=b),
        grid_spec=grid_spec,
        out_shape=jax.ShapeDtypeStruct((b, s, BRANCH_W), BF16),
        compiler_params=pltpu.CompilerParams(dimension_semantics=("arbitrary",), vmem_limit_bytes=VMEM_LIMIT),
        name="prompt_flash",
    )(qi_tab, ki_tab, qa, ka, va)


def _merge_kernel(h_ref, ya_ref, sga_ref, sma_ref, mbc_ref, wa_ref, wo_ref, g_ref, b_ref, o_ref, *, alpha):
    oa = _dot((ya_ref[...].astype(F32) * sga_ref[...].astype(F32)).astype(BF16), wa_ref[...])
    m = sma_ref[...].astype(F32) * oa + mbc_ref[...].astype(F32)
    x = alpha * h_ref[...] + _dot(m.astype(BF16), wo_ref[...])
    o_ref[...] = _layer_norm(x, g_ref[...], b_ref[...])


def _merge(h, ya, sga, sma, mbc, wa, wo, g, b, alpha, tm):
    n, d = h.shape
    rows = lambda w: pl.BlockSpec((tm, w), lambda i: (i, 0))
    const = lambda shape: pl.BlockSpec(shape, lambda i: (0,) * len(shape))
    return pl.pallas_call(
        functools.partial(_merge_kernel, alpha=alpha),
        grid=(n // tm,),
        in_specs=[rows(d), rows(BRANCH_W), rows(BRANCH_W), rows(d), rows(d),
                  const((BRANCH_W, d)), const((d, d)), const((1, d)), const((1, d))],
        out_specs=rows(d),
        out_shape=jax.ShapeDtypeStruct((n, d), F32),
        compiler_params=pltpu.CompilerParams(dimension_semantics=("arbitrary",), vmem_limit_bytes=VMEM_LIMIT),
        name="merge",
    )(h, ya, sga, sma, mbc, wa, wo, g, b)


def _sample_proj_kernel(h_ref, w1_ref, w1t_ref, bf_ref, cw_ref, lvg_ref, lvb_ref, wst_ref, bsr_ref, wb_ref, wc_ref,
                        zp1_ref, zp2_ref,
                        q_ref, k_ref, v_ref, lf_ref, cnt_ref, z_ref, vn_ref, sga_ref, sma_ref, mbc_ref,
                        p_ref, zbuf_ref, *, tn, n_steps, t_new, d_model):
    off, _ = _offsets(d_model)
    j = pl.program_id(0)
    ns = h_ref.shape[0]
    hb = h_ref[...].astype(BF16)
    col = pl.multiple_of(j * tn, LANES)
    p_ref[:, pl.ds(col, tn)] = _dot(hb, w1_ref[...])

    @pl.when(j == n_steps - 1)
    def _():
        def proj(name, width=BRANCH_W):
            return p_ref[:, off[name]:off[name] + width]

        qv = _dot_nt(hb, w1t_ref[...])
        q_ref[...] = qv[:, 0:BRANCH_W]
        v_ref[...] = qv[:, BRANCH_W:2 * BRANCH_W]
        k_ref[...] = proj("k")
        logf = _log_sigmoid(qv[:, 2 * BRANCH_W:2 * BRANCH_W + FORGET_PAD] + bf_ref[...])
        lf_ref[...] = logf[:, :N_HEADS]
        lt = logf.T
        tpos = lax.broadcasted_iota(jnp.int32, lt.shape, 1) % t_new
        cn = lt
        shift = 1
        while shift < t_new:
            cn = cn + jnp.where(tpos >= shift, pltpu.roll(cn, shift, axis=1), 0.0)
            shift *= 2
        cnt_ref[...] = cn[0:N_HEADS, :]

        sga_ref[...] = _silu(proj("ga")).astype(BF16)

        z = proj("cc") * proj("ch")
        z_ref[...] = z
        zbuf_ref[0:SUBLANES, :] = jnp.zeros((SUBLANES, BRANCH_W), F32)
        zbuf_ref[SUBLANES:SUBLANES + ns, :] = z
        trow = lax.broadcasted_iota(jnp.int32, (ns, BRANCH_W), 0) % t_new
        z1 = jnp.where(trow < 1, zp1_ref[...], zbuf_ref[SUBLANES - 1:SUBLANES - 1 + ns, :])
        z2 = jnp.where(trow < 2, zp2_ref[...], zbuf_ref[SUBLANES - 2:SUBLANES - 2 + ns, :])
        ob = _dot(_conv_gate(z, z1, z2, proj("cb"), proj("gb"), cw_ref), wb_ref[...])

        vn = _layer_norm(proj("vc"), lvg_ref[...], lvb_ref[...])
        vn_ref[...] = vn
        vnb = vn.astype(BF16)
        r = lax.broadcasted_iota(jnp.int32, (ns, ns), 0)
        c = lax.broadcasted_iota(jnp.int32, (ns, ns), 1)
        keep = (r // t_new == c // t_new) & (c % t_new <= r % t_new)
        s = jnp.concatenate(
            [_dot(jnp.where(keep, wst_ref[g], 0.0).astype(BF16), vnb[:, g * GROUP_DIM:(g + 1) * GROUP_DIM])
             + bsr_ref[:, g:g + 1] for g in range(N_GROUPS)], axis=1)
        oc = _dot((proj("u") * s * _silu(proj("gc"))).astype(BF16), wc_ref[...])

        sma_ref[...] = _sigmoid(proj("ma", d_model)).astype(BF16)
        mbc_ref[...] = (_sigmoid(proj("mb", d_model)) * ob + _sigmoid(proj("mc", d_model)) * oc).astype(BF16)


def _sample_proj(h, w1, w1t, bf, cw, lvg, lvb, wst, bsr, wb, wc, zp1, zp2, t_new):
    ns, d = h.shape
    wtot = w1.shape[1]
    tn = W1_ALIGN
    n_steps = wtot // tn
    const = lambda shape: pl.BlockSpec(shape, lambda j: (0,) * len(shape))
    f32 = lambda w: jax.ShapeDtypeStruct((ns, w), F32)
    bf16 = lambda w: jax.ShapeDtypeStruct((ns, w), BF16)
    return pl.pallas_call(
        functools.partial(_sample_proj_kernel, tn=tn, n_steps=n_steps, t_new=t_new, d_model=d),
        grid=(n_steps,),
        in_specs=[const((ns, d)), pl.BlockSpec((d, tn), lambda j: (0, j)), const((W1T_ROWS, d)),
                  const((1, FORGET_PAD)),
                  const((CONV_WIDTH, BRANCH_W)), const((1, BRANCH_W)), const((1, BRANCH_W)),
                  const((N_GROUPS, ns, ns)), const((ns, N_GROUPS)), const((BRANCH_W, d)), const((BRANCH_W, d)),
                  const((ns, BRANCH_W)), const((ns, BRANCH_W))],
        out_specs=[const((ns, BRANCH_W)), const((ns, BRANCH_W)), const((ns, BRANCH_W)), const((ns, N_HEADS)),
                   const((N_HEADS, ns)), const((ns, BRANCH_W)), const((ns, BRANCH_W)),
                   const((ns, BRANCH_W)), const((ns, d)), const((ns, d))],
        out_shape=[f32(BRANCH_W), f32(BRANCH_W), f32(BRANCH_W), f32(N_HEADS),
                   jax.ShapeDtypeStruct((N_HEADS, ns), F32), f32(BRANCH_W), f32(BRANCH_W),
                   bf16(BRANCH_W), bf16(d), bf16(d)],
        scratch_shapes=[pltpu.VMEM((ns, wtot), F32), pltpu.VMEM((ns + SUBLANES, BRANCH_W), F32)],
        compiler_params=pltpu.CompilerParams(dimension_semantics=("arbitrary",), vmem_limit_bytes=VMEM_LIMIT),
        name="sample_proj",
    )(h, w1, w1t, bf, cw, lvg, lvb, wst, bsr, wb, wc, zp1, zp2)


def _sample_attn_kernel(rows_ref, q_ref, kn_ref, vn_ref, cn_ref, ck_hbm, cv_hbm, clf_hbm, o_ref,
                        kbuf, vbuf, lbuf, sem, m_ref, l_ref, acc_ref, car_ref, *, pp, t_new):
    step = pl.program_id(1)
    n_steps = pl.num_programs(1)
    g = pl.program_id(0) * n_steps + step
    total = pl.num_programs(0) * n_steps
    slot = g % 2
    tpad = q_ref.shape[2]

    def page_copies(gg, sl):
        out = []
        for i in range(pp):
            row = rows_ref[gg * pp + i]
            out.append(pltpu.make_async_copy(ck_hbm.at[row], kbuf.at[sl, i], sem.at[sl, 0]))
            out.append(pltpu.make_async_copy(cv_hbm.at[row], vbuf.at[sl, i], sem.at[sl, 1]))
            out.append(pltpu.make_async_copy(clf_hbm.at[row], lbuf.at[sl, i], sem.at[sl, 2]))
        return out

    @pl.when(g == 0)
    def _():
        for c in page_copies(0, 0):
            c.start()

    @pl.when(g + 1 < total)
    def _():
        for c in page_copies(g + 1, 1 - slot):
            c.start()

    @pl.when(step == 0)
    def _():
        car_ref[...] = jnp.zeros_like(car_ref)
        q3 = q_ref[0] * SCALE
        trow = lax.broadcasted_iota(jnp.int32, (N_HEADS, tpad, 1), 1)
        ss = []
        for t in range(t_new):
            st = jnp.sum(q3 * kn_ref[0, :, t:t + 1, :], axis=2, keepdims=True) - cn_ref[0, :, t:t + 1][:, :, None]
            ss.append(jnp.where(trow >= t, st, -jnp.inf))
        m = ss[0]
        for st in ss[1:]:
            m = jnp.maximum(m, st)
        l = jnp.zeros((N_HEADS, tpad, 1), F32)
        acc = jnp.zeros((N_HEADS, tpad, HEAD_DIM), F32)
        for t in range(t_new):
            pt = jnp.exp(ss[t] - m)
            l = l + pt
            acc = acc + pt * vn_ref[0, :, t:t + 1, :]
        m_ref[...] = m
        l_ref[...] = l
        acc_ref[...] = acc

    for c in page_copies(g, slot):
        c.wait()

    ci = lax.broadcasted_iota(jnp.int32, (LANES, 2 * LANES), 0)
    cj = lax.broadcasted_iota(jnp.int32, (LANES, 2 * LANES), 1)
    usum = jnp.where((cj >= LANES) | (ci > cj), 1.0, 0.0).astype(BF16)
    carry = car_ref[...]
    bias = []
    for i in range(pp):
        r = jnp.zeros((N_HEADS, 2 * LANES), F32)
        for piece in _split3(lbuf[slot, i]):
            r = r + _dot(piece, usum)
        bias.append(r[:, :LANES] + carry)
        carry = carry + r[:, LANES:]
    car_ref[...] = carry
    bias = jnp.concatenate(bias, axis=1)

    ss = []
    for h in range(N_HEADS):
        qh = (q_ref[0, h] * SCALE).astype(BF16)
        kt = jnp.concatenate([kbuf[slot, i, h * HEAD_DIM:(h + 1) * HEAD_DIM, :] for i in range(pp)],
                             axis=1).astype(BF16)
        ss.append(_dot(qh, kt) + bias[h:h + 1, :])
    ps, al = [], []
    for h in range(N_HEADS):
        m_old = m_ref[h]
        m_new = jnp.maximum(m_old, jnp.max(ss[h], axis=1, keepdims=True))
        a = jnp.exp(m_old - m_new)
        p = jnp.exp(ss[h] - m_new)
        l_ref[h] = a * l_ref[h] + jnp.sum(p, axis=1, keepdims=True)
        m_ref[h] = m_new
        ps.append(p.astype(BF16))
        al.append(a)
    for h in range(N_HEADS):
        vt = jnp.concatenate([vbuf[slot, i, h * HEAD_DIM:(h + 1) * HEAD_DIM, :] for i in range(pp)],
                             axis=1).astype(BF16)
        acc_ref[h] = al[h] * acc_ref[h] + _dot_nt(ps[h], vt)

    @pl.when(step == n_steps - 1)
    def _():
        for h in range(N_HEADS):
            o_ref[0, h] = acc_ref[h] * (1.0 / l_ref[h])


def _sample_attn(rows, q, kn, vn, cn, cache_kt, cache_vt, cache_lft, n_pages, t_new):
    nb, tpad = q.shape[0], q.shape[2]
    width, page = cache_kt.shape[1], cache_kt.shape[2]
    pp = PAGES_PER_STEP if n_pages % PAGES_PER_STEP == 0 else n_pages
    n_steps = n_pages // pp
    per_seq = lambda shape: pl.BlockSpec((1,) + shape, lambda b, s, rref: (b,) + (0,) * len(shape))
    in_hbm = pl.BlockSpec(memory_space=pl.ANY)
    grid_spec = pltpu.PrefetchScalarGridSpec(
        num_scalar_prefetch=1,
        grid=(nb, n_steps),
        in_specs=[per_seq((N_HEADS, tpad, HEAD_DIM))] * 3 + [per_seq((N_HEADS, LANES))] + [in_hbm] * 3,
        out_specs=per_seq((N_HEADS, tpad, HEAD_DIM)),
        scratch_shapes=[pltpu.VMEM((2, pp, width, page), F32), pltpu.VMEM((2, pp, width, page), F32),
                        pltpu.VMEM((2, pp, N_HEADS, page), F32), pltpu.SemaphoreType.DMA((2, 3)),
                        pltpu.VMEM((N_HEADS, tpad, 1), F32), pltpu.VMEM((N_HEADS, tpad, 1), F32),
                        pltpu.VMEM((N_HEADS, tpad, HEAD_DIM), F32), pltpu.VMEM((N_HEADS, LANES), F32)],
    )
    return pl.pallas_call(
        functools.partial(_sample_attn_kernel, pp=pp, t_new=t_new),
        grid_spec=grid_spec,
        out_shape=jax.ShapeDtypeStruct((nb, N_HEADS, tpad, HEAD_DIM), F32),
        compiler_params=pltpu.CompilerParams(dimension_semantics=("arbitrary", "arbitrary"),
                                             vmem_limit_bytes=VMEM_LIMIT),
        name="sample_attn",
    )(rows, q, kn, vn, cn, cache_kt, cache_vt, cache_lft)


def _pick_tile(n, pref):
    t = pref
    while n % t:
        t //= 2
    return t


def kernel(x_prompt, x_sample, cache_k, cache_v, cache_logf, state_conv, page_table, ln_in_g, ln_in_b, w_in, b_f,
           conv_w, ln_v_g, ln_v_b, w_s, b_s, w_a_out, w_b_out, w_c_out, w_o, ln_g, ln_b):
    depth, d = w_in.shape[0], w_in.shape[1]
    bsz, seq, _ = x_prompt.shape
    nb, t_new, _ = x_sample.shape
    ns = nb * t_new
    assert seq % CHUNK == 0 and t_new <= SUBLANES and page_table.shape[0] == nb
    alpha = (2.0 * depth) ** 0.25
    tm = _pick_tile(seq, 512)
    tq = _pick_tile(seq, 512)

    _, wtot = _offsets(d)
    q0, k0, v0, f0, g0 = 0, BRANCH_W, 2 * BRANCH_W, 3 * BRANCH_W, 3 * BRANCH_W + N_HEADS
    used = w_in.shape[2] - 2 * BRANCH_W - N_HEADS
    w1 = jnp.concatenate([w_in[..., k0:v0], w_in[..., g0:], jnp.zeros((depth, d, wtot - used), F32)],
                         axis=-1).astype(BF16)
    w1t = jnp.swapaxes(jnp.concatenate([w_in[..., q0:k0], w_in[..., v0:f0],
                                        jnp.pad(w_in[..., f0:g0], ((0, 0), (0, 0), (0, FORGET_PAD - N_HEADS)))],
                                       axis=-1), 1, 2).astype(BF16)
    b_f_pad = jnp.pad(b_f, ((0, 0), (0, FORGET_PAD - N_HEADS)))
    bfr = b_f_pad.reshape(depth, 1, FORGET_PAD)
    bfc = jnp.broadcast_to(b_f_pad[:, :BF16_ROWS, None], (depth, BF16_ROWS, LANES))
    wa, wb, wc, wo = (w.astype(BF16) for w in (w_a_out, w_b_out, w_c_out, w_o))
    bst = jnp.swapaxes(b_s, 1, 2)
    wst = jnp.tile(w_s[:, :, :t_new, :t_new], (1, 1, ns // t_new, ns // t_new))
    bsr = jnp.tile(jnp.swapaxes(b_s[:, :, :t_new], 1, 2), (1, ns // t_new, 1))
    pool, page = cache_k.shape[1], cache_k.shape[2]
    n_pages = page_table.shape[1]
    cache_kt = jnp.transpose(cache_k, (0, 1, 3, 4, 2)).reshape(depth * pool, BRANCH_W, page)
    cache_vt = jnp.transpose(cache_v, (0, 1, 3, 4, 2)).reshape(depth * pool, BRANCH_W, page)
    cache_lft = jnp.swapaxes(cache_logf, 2, 3).reshape(depth * pool, N_HEADS, page)
    pages_latest_first = page_table[:, ::-1].reshape(-1)
    zp1 = jnp.concatenate([state_conv[:, :, 1:2], jnp.zeros((depth, nb, t_new - 1, BRANCH_W), F32)], axis=2)
    zp2 = jnp.concatenate([state_conv, jnp.zeros((depth, nb, t_new - 2, BRANCH_W), F32)], axis=2)
    zp1, zp2 = zp1.reshape(depth, ns, BRANCH_W), zp2.reshape(depth, ns, BRANCH_W)

    hp = _ln_rows(x_prompt.reshape(bsz * seq, d), ln_in_g, ln_in_b, tm).reshape(bsz, seq, d)
    hs = _ln_rows(x_sample.reshape(ns, d), ln_in_g, ln_in_b, ns)

    outs = {n: [] for n in ("kp", "vp", "lp", "cp", "ks", "vs", "ls", "cs", "us")}
    for l in range(depth):
        row1 = lambda a: a[l].reshape(1, -1)
        qa, ka, va, ko, vo, lf, tail, sga, sma, mbc = _prompt_proj(
            hp, w1[l], w1t[l], bfc[l], conv_w[l], row1(ln_v_g), row1(ln_v_b), w_s[l], bst[l], wb[l], wc[l], tm)
        ya = _flash(qa, ka, va, tq, tq)
        flat = lambda a: a.reshape(bsz * seq, a.shape[-1])
        hp = _merge(flat(hp), flat(ya), flat(sga), flat(sma), flat(mbc), wa[l], wo[l], row1(ln_g), row1(ln_b),
                    alpha, tm).reshape(bsz, seq, d)
        outs["kp"].append(ko); outs["vp"].append(vo); outs["lp"].append(lf); outs["cp"].append(tail)
        qs, ks, vs, lfs, cnt, zs, vns, sga, sma, mbc = _sample_proj(
            hs, w1[l], w1t[l], bfr[l], conv_w[l], row1(ln_v_g), row1(ln_v_b), wst[l], bsr[l], wb[l], wc[l],
            zp1[l], zp2[l], t_new)
        head_major = lambda a: jnp.pad(jnp.transpose(a.reshape(nb, t_new, N_HEADS, HEAD_DIM), (0, 2, 1, 3)),
                                       ((0, 0), (0, 0), (0, SUBLANES - t_new), (0, 0)))
        cn = jnp.pad(jnp.transpose(cnt.reshape(N_HEADS, nb, t_new), (1, 0, 2)), ((0, 0), (0, 0), (0, LANES - t_new)))
        yas = _sample_attn(pages_latest_first + l * pool, head_major(qs), head_major(ks), head_major(vs),
                           cn, cache_kt, cache_vt, cache_lft, n_pages, t_new)
        yas = jnp.transpose(yas[:, :, :t_new], (0, 2, 1, 3)).reshape(ns, BRANCH_W)
        hs = _merge(hs, yas.astype(BF16), sga, sma, mbc, wa[l], wo[l], row1(ln_g), row1(ln_b), alpha, ns)
        outs["ks"].append(ks); outs["vs"].append(vs); outs["ls"].append(lfs)
        outs["cs"].append(zs.reshape(nb, t_new, BRANCH_W)[:, t_new - (CONV_WIDTH - 1):])
        outs["us"].append(vns)

    st = {n: jnp.stack(v) for n, v in outs.items()}
    new_v_prompt = jnp.transpose(st["vp"].reshape(depth, bsz, N_HEADS, HEAD_DIM, seq), (0, 1, 4, 2, 3))
    return (hp, hs.reshape(nb, t_new, d),
            st["kp"].reshape(depth, bsz, seq, N_HEADS, HEAD_DIM), new_v_prompt,
            jnp.swapaxes(st["lp"], 2, 3), st["cp"],
            st["ks"].reshape(depth, nb, t_new, N_HEADS, HEAD_DIM), st["vs"].reshape(depth, nb, t_new, N_HEADS, HEAD_DIM),
            st["ls"].reshape(depth, nb, t_new, N_HEADS), st["cs"], st["us"].reshape(depth, nb, t_new, BRANCH_W))
```

```python
import functools
import math

import numpy as np
import jax
import jax.numpy as jnp
from jax import lax
from jax.experimental import pallas as pl
from jax.experimental.pallas import tpu as pltpu

F32 = jnp.float32
BF16 = jnp.bfloat16

N_HEADS = 8
HEAD_DIM = 64
BRANCH_W = 512
CHUNK = 128
N_GROUPS = 4
GROUP_DIM = BRANCH_W // N_GROUPS
CONV_WIDTH = 3
LN_EPS = 1e-5
LANES = 128
SUBLANES = 8
BF16_ROWS = 16
FORGET_PAD = LANES
ROW_SLOTS = ("k", "ga", "cb", "cc", "ch", "gb", "u", "vc", "gc")
W1_ALIGN = 2048
W1_CHUNK = 256
W1T_ROWS = 2 * BRANCH_W + W1_CHUNK
SCALE = HEAD_DIM ** -0.5
LOG2E = math.log2(math.e)
VMEM_LIMIT = 56 * 1024 * 1024
PAGES_PER_STEP = 16

AUG0 = HEAD_DIM
V_ROWS = HEAD_DIM + BF16_ROWS


def _offsets(d_model):
    off = {n: i * BRANCH_W for i, n in enumerate(ROW_SLOTS)}
    base = len(ROW_SLOTS) * BRANCH_W
    off["ma"], off["mb"], off["mc"] = base, base + d_model, base + 2 * d_model
    used = base + 3 * d_model
    return off, -(-used // W1_ALIGN) * W1_ALIGN


def _dot(a, b):
    return jnp.dot(a, b, preferred_element_type=F32)


def _dot_nt(a, b):
    return lax.dot_general(a, b, (((1,), (1,)), ((), ())), preferred_element_type=F32)


def _split3(x):
    hi = x.astype(BF16)
    r = x - hi.astype(F32)
    mid = r.astype(BF16)
    lo = (r - mid.astype(F32)).astype(BF16)
    return hi, mid, lo


def _sigmoid(x):
    return 1.0 / (1.0 + jnp.exp(-x))


def _silu(x):
    return x * _sigmoid(x)


def _log_sigmoid(x):
    return jnp.minimum(x, 0.0) - jnp.log1p(jnp.exp(-jnp.abs(x)))


def _layer_norm(x, g, b):
    mu = jnp.mean(x, axis=-1, keepdims=True)
    xc = x - mu
    var = jnp.mean(xc * xc, axis=-1, keepdims=True)
    return xc * lax.rsqrt(var + LN_EPS) * g + b


def _head_tiles(x):
    out = []
    for j in range(BRANCH_W // LANES):
        blk = x[:, j * LANES:(j + 1) * LANES]
        out.append(blk)
        out.append(pltpu.roll(blk, HEAD_DIM, axis=1))
    return out


def _ln_kernel(x_ref, g_ref, b_ref, o_ref):
    o_ref[...] = _layer_norm(x_ref[...], g_ref[...], b_ref[...])


def _ln_rows(x, g, b, tm):
    n, d = x.shape
    return pl.pallas_call(
        _ln_kernel,
        grid=(n // tm,),
        in_specs=[pl.BlockSpec((tm, d), lambda i: (i, 0)),
                  pl.BlockSpec((1, d), lambda i: (0, 0)),
                  pl.BlockSpec((1, d), lambda i: (0, 0))],
        out_specs=pl.BlockSpec((tm, d), lambda i: (i, 0)),
        out_shape=jax.ShapeDtypeStruct((n, d), F32),
        name="ln_in",
    )(x, g.reshape(1, d), b.reshape(1, d))


def _w1_kernel(a_ref, b_ref, o_ref, *, n_k, n_real, shift):
    j = pl.program_id(1)

    @pl.when(j < n_k)
    def _():
        o_ref[0] = a_ref[0].T.astype(BF16)

    @pl.when((j >= n_k) & (j < n_real))
    def _():
        o_ref[0] = jnp.concatenate([a_ref[0, shift:, :], b_ref[0, :shift, :]], axis=0).T.astype(BF16)

    @pl.when(j >= n_real)
    def _():
        o_ref[0] = jnp.zeros(o_ref.shape[1:], BF16)


def _w1t_kernel(a_ref, o_ref, *, n_chunks):
    c = pl.program_id(1)
    x = a_ref[0]
    keep = (c < n_chunks - 1) | (lax.broadcasted_iota(jnp.int32, x.shape, 0) < N_HEADS)
    o_ref[0] = jnp.where(keep, x, 0.0).astype(BF16)


def _prep_weights(w_int, wtot):
    depth, p_total, d = w_int.shape
    k0, v0, f0 = BRANCH_W, 2 * BRANCH_W, 3 * BRANCH_W
    g0 = f0 + N_HEADS
    n_k = BRANCH_W // W1_CHUNK
    n_real = n_k + (p_total - g0) // W1_CHUNK
    shift = g0 % W1_CHUNK
    assert (p_total - g0) % W1_CHUNK == 0 and wtot % W1_CHUNK == 0 and 0 < shift and shift % SUBLANES == 0
    last_blk = (p_total - 1) // W1_CHUNK

    def blk_a(j):
        jj = jnp.minimum(j, n_real - 1)
        return jnp.where(jj < n_k, jj + k0 // W1_CHUNK, jj - n_k + g0 // W1_CHUNK)

    w1 = pl.pallas_call(
        functools.partial(_w1_kernel, n_k=n_k, n_real=n_real, shift=shift),
        grid=(depth, wtot // W1_CHUNK),
        in_specs=[pl.BlockSpec((1, W1_CHUNK, d), lambda l, j: (l, blk_a(j), 0)),
                  pl.BlockSpec((1, W1_CHUNK, d), lambda l, j: (l, jnp.minimum(blk_a(j) + 1, last_blk), 0))],
        out_specs=pl.BlockSpec((1, d, W1_CHUNK), lambda l, j: (l, 0, j)),
        out_shape=jax.ShapeDtypeStruct((depth, d, wtot), BF16),
        name="w1_rowmajor",
    )(w_int, w_int)
    n_chunks = W1T_ROWS // W1_CHUNK
    n_q = BRANCH_W // W1_CHUNK
    w1t = pl.pallas_call(
        functools.partial(_w1t_kernel, n_chunks=n_chunks),
        grid=(depth, n_chunks),
        in_specs=[pl.BlockSpec((1, W1_CHUNK, d),
                               lambda l, c: (l, jnp.where(c < n_q, c, c - n_q + v0 // W1_CHUNK), 0))],
        out_specs=pl.BlockSpec((1, W1_CHUNK, d), lambda l, c: (l, c, 0)),
        out_shape=jax.ShapeDtypeStruct((depth, W1T_ROWS, d), BF16),
        name="w1_transposed",
    )(w_int)
    return w1, w1t


def _conv_gate(z, z1, z2, cb, gb, cw_ref):
    y = cw_ref[0:1, :] * z2 + cw_ref[1:2, :] * z1 + cw_ref[2:3, :] * z
    return (cb * y * _silu(gb)).astype(BF16)


def _prompt_proj_kernel(layer_ref, *refs, tm, d_model, n_in):
    del layer_ref
    (h_ref, w1_ref, w1t_ref, bfc_ref, cw_ref, lvg_ref, lvb_ref, ws_ref, bst_ref, wb_ref, wc_ref) = refs[:11]
    (qa_ref, ka_ref, va_ref, ko_ref, vo_ref, lf_ref, tail_ref, sga_ref, sma_ref, mbc_ref,
     ccol_ref, zbuf_ref) = refs[n_in:]
    off, _ = _offsets(d_model)
    i = pl.program_id(1)

    @pl.when(i == 0)
    def _():
        ccol_ref[...] = jnp.zeros_like(ccol_ref)
        zbuf_ref[0:SUBLANES, :] = jnp.zeros((SUBLANES, BRANCH_W), F32)

    hb = h_ref[0].astype(BF16)

    def proj(name, width=BRANCH_W):
        return _dot(hb, w1_ref[:, off[name]:off[name] + width])

    pt = _dot_nt(w1t_ref[0:2 * BRANCH_W + BF16_ROWS, :], hb)

    logf_c = _log_sigmoid(pt[2 * BRANCH_W:2 * BRANCH_W + BF16_ROWS] + bfc_ref[:, 0:1])
    lf_ref[0] = logf_c[0:N_HEADS]
    upper = jnp.where(lax.broadcasted_iota(jnp.int32, (tm, tm), 0) <= lax.broadcasted_iota(jnp.int32, (tm, tm), 1),
                      1.0, 0.0).astype(BF16)
    c_c = ccol_ref[:, 0:1]
    for piece in _split3(logf_c):
        c_c = c_c + _dot(piece, upper)
    ccol_ref[...] = jnp.broadcast_to(c_c[:, tm - 1:tm], ccol_ref.shape)
    c_c = c_c * LOG2E
    c3c = [p.astype(F32) for p in _split3(c_c)]
    c_r = jnp.concatenate([c_c, jnp.zeros((LANES - BF16_ROWS, tm), F32)], axis=0).T
    c3r = [p.astype(F32) for p in _split3(c_r)]

    k = proj("k")
    ko_ref[0, 0] = k
    vo_ref[0, 0] = pt[BRANCH_W:2 * BRANCH_W]
    kh = _head_tiles(k)
    lane = lax.broadcasted_iota(jnp.int32, (tm, LANES), 1)
    row8 = lax.broadcasted_iota(jnp.int32, (SUBLANES, tm), 0)
    zpad = jnp.zeros((LANES - HEAD_DIM - SUBLANES, tm), F32)
    vpad = jnp.zeros((V_ROWS - HEAD_DIM - SUBLANES, tm), F32)
    ones_row = jnp.where(row8 == 0, 1.0, 0.0)
    for h in range(N_HEADS):
        cb3 = [jnp.broadcast_to(p[:, h:h + 1], (tm, LANES)) for p in c3r]
        ak = jnp.where(lane < AUG0 + 3, 1.0, jnp.where(lane == AUG0 + 3, -cb3[0], jnp.where(
            lane == AUG0 + 4, -cb3[1], jnp.where(lane == AUG0 + 5, -cb3[2], 0.0))))
        ka_ref[0, h] = jnp.where(lane < HEAD_DIM, kh[h], ak).astype(BF16)
        aq = jnp.where(row8 == 0, c3c[0][h:h + 1], jnp.where(row8 == 1, c3c[1][h:h + 1], jnp.where(
            row8 == 2, c3c[2][h:h + 1], jnp.where(row8 < 6, 1.0, 0.0))))
        qt = pt[h * HEAD_DIM:(h + 1) * HEAD_DIM] * (SCALE * LOG2E)
        qa_ref[0, h] = jnp.concatenate([qt, aq, zpad], axis=0).astype(BF16)
        vt = pt[BRANCH_W + h * HEAD_DIM:BRANCH_W + (h + 1) * HEAD_DIM]
        va_ref[0, h] = jnp.concatenate([vt, ones_row, vpad], axis=0).astype(BF16)

    sga_ref[0] = _silu(proj("ga")).astype(BF16)

    z = proj("cc") * proj("ch")
    zbuf_ref[SUBLANES:SUBLANES + tm, :] = z
    z1 = zbuf_ref[SUBLANES - 1:SUBLANES - 1 + tm, :]
    z2 = zbuf_ref[SUBLANES - 2:SUBLANES - 2 + tm, :]
    zbuf_ref[0:SUBLANES, :] = z[tm - SUBLANES:tm, :]
    tail_ref[0] = z[tm - (CONV_WIDTH - 1):tm, :]
    ob = _dot(_conv_gate(z, z1, z2, proj("cb"), proj("gb"), cw_ref), wb_ref[...])

    vn = _layer_norm(proj("vc"), lvg_ref[...], lvb_ref[...]).astype(BF16)
    tril = lax.broadcasted_iota(jnp.int32, (CHUNK, CHUNK), 0) >= lax.broadcasted_iota(jnp.int32, (CHUNK, CHUNK), 1)
    wm = [jnp.where(tril, ws_ref[g], 0.0).astype(BF16) for g in range(N_GROUPS)]
    rows = []
    for cidx in range(tm // CHUNK):
        rows.append(jnp.concatenate(
            [_dot(wm[g], vn[cidx * CHUNK:(cidx + 1) * CHUNK, g * GROUP_DIM:(g + 1) * GROUP_DIM]) + bst_ref[:, g:g + 1]
             for g in range(N_GROUPS)], axis=1))
    s = jnp.concatenate(rows, axis=0)
    oc = _dot((proj("u") * s * _silu(proj("gc"))).astype(BF16), wc_ref[...])

    sma_ref[0] = _sigmoid(proj("ma", d_model)).astype(BF16)
    mbc_ref[0] = (_sigmoid(proj("mb", d_model)) * ob + _sigmoid(proj("mc", d_model)) * oc).astype(BF16)


def _prompt_proj(layer, depth, kv_all, h, w1, w1t, bfc, cw, lvg, lvb, ws, bst, wb, wc, tm):
    b, s, d = h.shape
    wtot = w1.shape[2]
    const = lambda shape: pl.BlockSpec(shape, lambda bi, i, lr: (0,) * len(shape), pipeline_mode=pl.Buffered(1))
    per_layer = lambda shape: pl.BlockSpec((None,) + shape, lambda bi, i, lr: (lr[0],) + (0,) * len(shape),
                                           pipeline_mode=pl.Buffered(1))
    rows = lambda w: pl.BlockSpec((1, tm, w), lambda bi, i, lr: (bi, i, 0))
    cols = lambda r: pl.BlockSpec((1, r, tm), lambda bi, i, lr: (bi, 0, i))
    heads_r = pl.BlockSpec((1, N_HEADS, tm, LANES), lambda bi, i, lr: (bi, 0, i, 0))
    heads_c = lambda r: pl.BlockSpec((1, N_HEADS, r, tm), lambda bi, i, lr: (bi, 0, 0, i))
    in_specs = [rows(d), per_layer((d, wtot)), per_layer((W1T_ROWS, d)),
                const((BF16_ROWS, LANES)), const((CONV_WIDTH, BRANCH_W)),
                const((1, BRANCH_W)), const((1, BRANCH_W)), const((N_GROUPS, CHUNK, CHUNK)),
                const((CHUNK, N_GROUPS)), const((BRANCH_W, d)), const((BRANCH_W, d))]
    args = [h, w1, w1t, bfc, cw, lvg, lvb, ws, bst, wb, wc]
    n_in = 1 + len(args)
    in_specs += [pl.BlockSpec(memory_space=pl.ANY)] * 2
    args += list(kv_all)
    aliases = {n_in: 3, n_in + 1: 4}
    grid_spec = pltpu.PrefetchScalarGridSpec(
        num_scalar_prefetch=1,
        grid=(b, s // tm),
        in_specs=in_specs,
        out_specs=[heads_c(LANES), heads_r, heads_c(V_ROWS),
                   pl.BlockSpec((1, 1, tm, BRANCH_W), lambda bi, i, lr: (lr[0], bi, i, 0)),
                   pl.BlockSpec((1, 1, BRANCH_W, tm), lambda bi, i, lr: (lr[0], bi, 0, i)),
                   cols(N_HEADS),
                   pl.BlockSpec((1, CONV_WIDTH - 1, BRANCH_W), lambda bi, i, lr: (bi, 0, 0)),
                   rows(BRANCH_W), rows(d), rows(d)],
        scratch_shapes=[pltpu.VMEM((BF16_ROWS, LANES), F32), pltpu.VMEM((tm + SUBLANES, BRANCH_W), F32)],
    )
    return pl.pallas_call(
        functools.partial(_prompt_proj_kernel, tm=tm, d_model=d, n_in=len(args)),
        grid_spec=grid_spec,
        out_shape=[jax.ShapeDtypeStruct((b, N_HEADS, LANES, s), BF16), jax.ShapeDtypeStruct((b, N_HEADS, s, LANES), BF16),
                   jax.ShapeDtypeStruct((b, N_HEADS, V_ROWS, s), BF16),
                   jax.ShapeDtypeStruct((depth, b, s, BRANCH_W), F32), jax.ShapeDtypeStruct((depth, b, BRANCH_W, s), F32),
                   jax.ShapeDtypeStruct((b, N_HEADS, s), F32),
                   jax.ShapeDtypeStruct((b, CONV_WIDTH - 1, BRANCH_W), F32),
                   jax.ShapeDtypeStruct((b, s, BRANCH_W), BF16), jax.ShapeDtypeStruct((b, s, d), BF16),
                   jax.ShapeDtypeStruct((b, s, d), BF16)],
        input_output_aliases=aliases,
        compiler_params=pltpu.CompilerParams(dimension_semantics=("arbitrary", "arbitrary"),
                                             vmem_limit_bytes=VMEM_LIMIT),
        name="prompt_proj",
    )(jnp.full((1,), layer, jnp.int32), *args)


def _flash_kernel(qi_ref, ki_ref, qt_ref, k_ref, vt_ref, o_ref, m_ref, acc_ref, *, tq, tk, bsz):
    p = pl.program_id(0)
    qi = qi_ref[p]
    ki = ki_ref[p]
    n_units = bsz * N_HEADS

    @pl.when(ki == 0)
    def _():
        m_ref[...] = jnp.full_like(m_ref, -jnp.inf)
        acc_ref[...] = jnp.zeros_like(acc_ref)

    def body(masked):
        if masked:
            mask = (ki * tk + lax.broadcasted_iota(jnp.int32, (tk, tq), 0)
                    <= qi * tq + lax.broadcasted_iota(jnp.int32, (tk, tq), 1))

        def scores(u):
            b, h = divmod(u, N_HEADS)
            s = _dot(k_ref[b, h], qt_ref[b, h])
            return jnp.where(mask, s, -jnp.inf) if masked else s

        def softmax(u, s):
            m_old = m_ref[u]
            m_new = jnp.maximum(m_old, jnp.max(s, axis=0, keepdims=True))
            m_ref[u] = m_new
            return jnp.exp2(s - m_new).astype(BF16), jnp.exp2(m_old - m_new)

        def accumulate(u, pexp, a):
            b, h = divmod(u, N_HEADS)
            acc_ref[u] = a * acc_ref[u] + _dot(vt_ref[b, h], pexp)

        s_next = scores(0)
        prev = None
        for u in range(n_units):
            s_cur = s_next
            if u + 1 < n_units:
                s_next = scores(u + 1)
            cur = softmax(u, s_cur)
            if prev is not None:
                accumulate(u - 1, *prev)
            prev = cur
        accumulate(n_units - 1, *prev)

    full = (ki + 1) * tk - 1 <= qi * tq

    @pl.when(full)
    def _():
        body(False)

    @pl.when(jnp.logical_not(full))
    def _():
        body(True)

    @pl.when(ki == (qi * tq + tq - 1) // tk)
    def _():
        for b in range(bsz):
            for j in range(N_HEADS // 2):
                a0, a1 = acc_ref[b * N_HEADS + 2 * j], acc_ref[b * N_HEADS + 2 * j + 1]
                yy = jnp.concatenate([a0[:HEAD_DIM] * (1.0 / a0[AUG0:AUG0 + 1]),
                                      a1[:HEAD_DIM] * (1.0 / a1[AUG0:AUG0 + 1])], axis=0)
                o_ref[b, :, j * LANES:(j + 1) * LANES] = yy.T.astype(BF16)


def _flash(qa, ka, va, tq, tk):
    b, _, s, _ = ka.shape
    pairs = [(qi, ki) for qi in range(s // tq) for ki in range((qi * tq + tq - 1) // tk + 1)]
    qi_tab = jnp.asarray(np.array([p[0] for p in pairs], np.int32))
    ki_tab = jnp.asarray(np.array([p[1] for p in pairs], np.int32))
    grid_spec = pltpu.PrefetchScalarGridSpec(
        num_scalar_prefetch=2,
        grid=(len(pairs),),
        in_specs=[pl.BlockSpec((b, N_HEADS, LANES, tq), lambda p, qt, kt: (0, 0, 0, qt[p])),
                  pl.BlockSpec((b, N_HEADS, tk, LANES), lambda p, qt, kt: (0, 0, kt[p], 0)),
                  pl.BlockSpec((b, N_HEADS, V_ROWS, tk), lambda p, qt, kt: (0, 0, 0, kt[p]))],
        out_specs=pl.BlockSpec((b, tq, BRANCH_W), lambda p, qt, kt: (0, qt[p], 0)),
        scratch_shapes=[pltpu.VMEM((b * N_HEADS, 1, tq), F32), pltpu.VMEM((b * N_HEADS, V_ROWS, tq), F32)],
    )
    return pl.pallas_call(
        functools.partial(_flash_kernel, tq=tq, tk=tk, bsz=b),
        grid_spec=grid_spec,
        out_shape=jax.ShapeDtypeStruct((b, s, BRANCH_W), BF16),
        compiler_params=pltpu.CompilerParams(dimension_semantics=("arbitrary",), vmem_limit_bytes=VMEM_LIMIT),
        name="prompt_flash",
    )(qi_tab, ki_tab, qa, ka, va)


def _merge_kernel(h_ref, ya_ref, sga_ref, sma_ref, mbc_ref, wa_ref, wo_ref, g_ref, b_ref, o_ref, *, alpha):
    oa = _dot((ya_ref[...].astype(F32) * sga_ref[...].astype(F32)).astype(BF16), wa_ref[...])
    m = sma_ref[...].astype(F32) * oa + mbc_ref[...].astype(F32)
    x = alpha * h_ref[...] + _dot(m.astype(BF16), wo_ref[...])
    o_ref[...] = _layer_norm(x, g_ref[...], b_ref[...])


def _merge(h, ya, sga, sma, mbc, wa, wo, g, b, alpha, tm):
    n, d = h.shape
    rows = lambda w: pl.BlockSpec((tm, w), lambda i: (i, 0))
    const = lambda shape: pl.BlockSpec(shape, lambda i: (0,) * len(shape))
    return pl.pallas_call(
        functools.partial(_merge_kernel, alpha=alpha),
        grid=(n // tm,),
        in_specs=[rows(d), rows(BRANCH_W), rows(BRANCH_W), rows(d), rows(d),
                  const((BRANCH_W, d)), const((d, d)), const((1, d)), const((1, d))],
        out_specs=rows(d),
        out_shape=jax.ShapeDtypeStruct((n, d), F32),
        compiler_params=pltpu.CompilerParams(dimension_semantics=("arbitrary",), vmem_limit_bytes=VMEM_LIMIT),
        name="merge",
    )(h, ya, sga, sma, mbc, wa, wo, g, b)


def _sample_proj_kernel(h_ref, w1_ref, w1t_ref, bf_ref, cw_ref, lvg_ref, lvb_ref, wst_ref, bsr_ref, wb_ref, wc_ref,
                        zp1_ref, zp2_ref,
                        q_ref, k_ref, v_ref, lf_ref, cnt_ref, z_ref, vn_ref, sga_ref, sma_ref, mbc_ref,
                        p_ref, zbuf_ref, *, tn, n_steps, t_new, d_model):
    off, _ = _offsets(d_model)
    j = pl.program_id(0)
    ns = h_ref.shape[0]
    hb = h_ref[...].astype(BF16)
    col = pl.multiple_of(j * tn, LANES)
    p_ref[:, pl.ds(col, tn)] = _dot(hb, w1_ref[...])

    @pl.when(j == n_steps - 1)
    def _():
        def proj(name, width=BRANCH_W):
            return p_ref[:, off[name]:off[name] + width]

        qv = _dot_nt(hb, w1t_ref[...])
        q_ref[...] = qv[:, 0:BRANCH_W]
        v_ref[...] = qv[:, BRANCH_W:2 * BRANCH_W]
        k_ref[...] = proj("k")
        logf = _log_sigmoid(qv[:, 2 * BRANCH_W:2 * BRANCH_W + FORGET_PAD] + bf_ref[...])
        lf_ref[...] = logf[:, :N_HEADS]
        lt = logf.T
        tpos = lax.broadcasted_iota(jnp.int32, lt.shape, 1) % t_new
        cn = lt
        shift = 1
        while shift < t_new:
            cn = cn + jnp.where(tpos >= shift, pltpu.roll(cn, shift, axis=1), 0.0)
            shift *= 2
        cnt_ref[...] = cn[0:N_HEADS, :]

        sga_ref[...] = _silu(proj("ga")).astype(BF16)

        z = proj("cc") * proj("ch")
        z_ref[...] = z
        zbuf_ref[0:SUBLANES, :] = jnp.zeros((SUBLANES, BRANCH_W), F32)
        zbuf_ref[SUBLANES:SUBLANES + ns, :] = z
        trow = lax.broadcasted_iota(jnp.int32, (ns, BRANCH_W), 0) % t_new
        z1 = jnp.where(trow < 1, zp1_ref[...], zbuf_ref[SUBLANES - 1:SUBLANES - 1 + ns, :])
        z2 = jnp.where(trow < 2, zp2_ref[...], zbuf_ref[SUBLANES - 2:SUBLANES - 2 + ns, :])
        ob = _dot(_conv_gate(z, z1, z2, proj("cb"), proj("gb"), cw_ref), wb_ref[...])

        vn = _layer_norm(proj("vc"), lvg_ref[...], lvb_ref[...])
        vn_ref[...] = vn
        vnb = vn.astype(BF16)
        r = lax.broadcasted_iota(jnp.int32, (ns, ns), 0)
        c = lax.broadcasted_iota(jnp.int32, (ns, ns), 1)
        keep = (r // t_new == c // t_new) & (c % t_new <= r % t_new)
        s = jnp.concatenate(
            [_dot(jnp.where(keep, wst_ref[g], 0.0).astype(BF16), vnb[:, g * GROUP_DIM:(g + 1) * GROUP_DIM])
             + bsr_ref[:, g:g + 1] for g in range(N_GROUPS)], axis=1)
        oc = _dot((proj("u") * s * _silu(proj("gc"))).astype(BF16), wc_ref[...])

        sma_ref[...] = _sigmoid(proj("ma", d_model)).astype(BF16)
        mbc_ref[...] = (_sigmoid(proj("mb", d_model)) * ob + _sigmoid(proj("mc", d_model)) * oc).astype(BF16)


def _sample_proj(layer, h, w1, w1t, bf, cw, lvg, lvb, wst, bsr, wb, wc, zp1, zp2, t_new):
    ns, d = h.shape
    wtot = w1.shape[2]
    tn = W1_ALIGN
    n_steps = wtot // tn
    const = lambda shape: pl.BlockSpec(shape, lambda j: (0,) * len(shape))
    f32 = lambda w: jax.ShapeDtypeStruct((ns, w), F32)
    bf16 = lambda w: jax.ShapeDtypeStruct((ns, w), BF16)
    return pl.pallas_call(
        functools.partial(_sample_proj_kernel, tn=tn, n_steps=n_steps, t_new=t_new, d_model=d),
        grid=(n_steps,),
        in_specs=[const((ns, d)), pl.BlockSpec((None, d, tn), lambda j: (layer, 0, j)),
                  pl.BlockSpec((None, W1T_ROWS, d), lambda j: (layer, 0, 0)), const((1, FORGET_PAD)),
                  const((CONV_WIDTH, BRANCH_W)), const((1, BRANCH_W)), const((1, BRANCH_W)),
                  const((N_GROUPS, ns, ns)), const((ns, N_GROUPS)), const((BRANCH_W, d)), const((BRANCH_W, d)),
                  const((ns, BRANCH_W)), const((ns, BRANCH_W))],
        out_specs=[const((ns, BRANCH_W)), const((ns, BRANCH_W)), const((ns, BRANCH_W)), const((ns, N_HEADS)),
                   const((N_HEADS, ns)), const((ns, BRANCH_W)), const((ns, BRANCH_W)),
                   const((ns, BRANCH_W)), const((ns, d)), const((ns, d))],
        out_shape=[f32(BRANCH_W), f32(BRANCH_W), f32(BRANCH_W), f32(N_HEADS),
                   jax.ShapeDtypeStruct((N_HEADS, ns), F32), f32(BRANCH_W), f32(BRANCH_W),
                   bf16(BRANCH_W), bf16(d), bf16(d)],
        scratch_shapes=[pltpu.VMEM((ns, wtot), F32), pltpu.VMEM((ns + SUBLANES, BRANCH_W), F32)],
        compiler_params=pltpu.CompilerParams(dimension_semantics=("arbitrary",), vmem_limit_bytes=VMEM_LIMIT),
        name="sample_proj",
    )(h, w1, w1t, bf, cw, lvg, lvb, wst, bsr, wb, wc, zp1, zp2)


def _sample_attn_kernel(rows_ref, q_ref, kn_ref, vn_ref, cn_ref, ck_hbm, cv_hbm, clf_hbm, o_ref,
                        kbuf, vbuf, lbuf, sem, m_ref, l_ref, acc_ref, car_ref, *, pp, t_new):
    step = pl.program_id(1)
    n_steps = pl.num_programs(1)
    g = pl.program_id(0) * n_steps + step
    total = pl.num_programs(0) * n_steps
    slot = g % 2
    tpad = q_ref.shape[2]

    def page_copies(gg, sl):
        out = []
        for i in range(pp):
            row = rows_ref[gg * pp + i]
            out.append(pltpu.make_async_copy(ck_hbm.at[row], kbuf.at[sl, i], sem.at[sl, 0]))
            out.append(pltpu.make_async_copy(cv_hbm.at[row], vbuf.at[sl, i], sem.at[sl, 1]))
            out.append(pltpu.make_async_copy(clf_hbm.at[row], lbuf.at[sl, i], sem.at[sl, 2]))
        return out

    @pl.when(g == 0)
    def _():
        for c in page_copies(0, 0):
            c.start()

    @pl.when(g + 1 < total)
    def _():
        for c in page_copies(g + 1, 1 - slot):
            c.start()

    @pl.when(step == 0)
    def _():
        car_ref[...] = jnp.zeros_like(car_ref)
        q3 = q_ref[0] * SCALE
        trow = lax.broadcasted_iota(jnp.int32, (N_HEADS, tpad, 1), 1)
        ss = []
        for t in range(t_new):
            st = jnp.sum(q3 * kn_ref[0, :, t:t + 1, :], axis=2, keepdims=True) - cn_ref[0, :, t:t + 1][:, :, None]
            ss.append(jnp.where(trow >= t, st, -jnp.inf))
        m = ss[0]
        for st in ss[1:]:
            m = jnp.maximum(m, st)
        l = jnp.zeros((N_HEADS, tpad, 1), F32)
        acc = jnp.zeros((N_HEADS, tpad, HEAD_DIM), F32)
        for t in range(t_new):
            pt = jnp.exp(ss[t] - m)
            l = l + pt
            acc = acc + pt * vn_ref[0, :, t:t + 1, :]
        m_ref[...] = m
        l_ref[...] = l
        acc_ref[...] = acc

    for c in page_copies(g, slot):
        c.wait()

    ci = lax.broadcasted_iota(jnp.int32, (LANES, 2 * LANES), 0)
    cj = lax.broadcasted_iota(jnp.int32, (LANES, 2 * LANES), 1)
    usum = jnp.where((cj >= LANES) | (ci > cj), 1.0, 0.0).astype(BF16)
    carry = car_ref[...]
    bias = []
    for i in range(pp):
        r = jnp.zeros((N_HEADS, 2 * LANES), F32)
        for piece in _split3(lbuf[slot, i]):
            r = r + _dot(piece, usum)
        bias.append(r[:, :LANES] + carry)
        carry = carry + r[:, LANES:]
    car_ref[...] = carry
    bias = jnp.concatenate(bias, axis=1)

    ss = []
    for h in range(N_HEADS):
        qh = (q_ref[0, h] * SCALE).astype(BF16)
        kt = jnp.concatenate([kbuf[slot, i, h * HEAD_DIM:(h + 1) * HEAD_DIM, :] for i in range(pp)],
                             axis=1).astype(BF16)
        ss.append(_dot(qh, kt) + bias[h:h + 1, :])
    ps, al = [], []
    for h in range(N_HEADS):
        m_old = m_ref[h]
        m_new = jnp.maximum(m_old, jnp.max(ss[h], axis=1, keepdims=True))
        a = jnp.exp(m_old - m_new)
        p = jnp.exp(ss[h] - m_new)
        l_ref[h] = a * l_ref[h] + jnp.sum(p, axis=1, keepdims=True)
        m_ref[h] = m_new
        ps.append(p.astype(BF16))
        al.append(a)
    for h in range(N_HEADS):
        vt = jnp.concatenate([vbuf[slot, i, h * HEAD_DIM:(h + 1) * HEAD_DIM, :] for i in range(pp)],
                             axis=1).astype(BF16)
        acc_ref[h] = al[h] * acc_ref[h] + _dot_nt(ps[h], vt)

    @pl.when(step == n_steps - 1)
    def _():
        for h in range(N_HEADS):
            o_ref[0, h] = acc_ref[h] * (1.0 / l_ref[h])


def _sample_attn(rows, q, kn, vn, cn, cache_kt, cache_vt, cache_lft, n_pages, t_new):
    nb, tpad = q.shape[0], q.shape[2]
    width, page = cache_kt.shape[1], cache_kt.shape[2]
    pp = PAGES_PER_STEP if n_pages % PAGES_PER_STEP == 0 else n_pages
    n_steps = n_pages // pp
    per_seq = lambda shape: pl.BlockSpec((1,) + shape, lambda b, s, rref: (b,) + (0,) * len(shape))
    in_hbm = pl.BlockSpec(memory_space=pl.ANY)
    grid_spec = pltpu.PrefetchScalarGridSpec(
        num_scalar_prefetch=1,
        grid=(nb, n_steps),
        in_specs=[per_seq((N_HEADS, tpad, HEAD_DIM))] * 3 + [per_seq((N_HEADS, LANES))] + [in_hbm] * 3,
        out_specs=per_seq((N_HEADS, tpad, HEAD_DIM)),
        scratch_shapes=[pltpu.VMEM((2, pp, width, page), F32), pltpu.VMEM((2, pp, width, page), F32),
                        pltpu.VMEM((2, pp, N_HEADS, page), F32), pltpu.SemaphoreType.DMA((2, 3)),
                        pltpu.VMEM((N_HEADS, tpad, 1), F32), pltpu.VMEM((N_HEADS, tpad, 1), F32),
                        pltpu.VMEM((N_HEADS, tpad, HEAD_DIM), F32), pltpu.VMEM((N_HEADS, LANES), F32)],
    )
    return pl.pallas_call(
        functools.partial(_sample_attn_kernel, pp=pp, t_new=t_new),
        grid_spec=grid_spec,
        out_shape=jax.ShapeDtypeStruct((nb, N_HEADS, tpad, HEAD_DIM), F32),
        compiler_params=pltpu.CompilerParams(dimension_semantics=("arbitrary", "arbitrary"),
                                             vmem_limit_bytes=VMEM_LIMIT),
        name="sample_attn",
    )(rows, q, kn, vn, cn, cache_kt, cache_vt, cache_lft)


def _pick_tile(n, pref):
    t = pref
    while n % t:
        t //= 2
    return t


def kernel(x_prompt, x_sample, cache_k, cache_v, cache_logf, state_conv, page_table, ln_in_g, ln_in_b, w_in, b_f,
           conv_w, ln_v_g, ln_v_b, w_s, b_s, w_a_out, w_b_out, w_c_out, w_o, ln_g, ln_b):
    depth, d = w_in.shape[0], w_in.shape[1]
    bsz, seq, _ = x_prompt.shape
    nb, t_new, _ = x_sample.shape
    ns = nb * t_new
    assert seq % CHUNK == 0 and t_new <= SUBLANES and page_table.shape[0] == nb
    alpha = (2.0 * depth) ** 0.25
    tm = _pick_tile(seq, 512)
    tq = _pick_tile(seq, 512)

    _, wtot = _offsets(d)
    q0, k0, v0, f0, g0 = 0, BRANCH_W, 2 * BRANCH_W, 3 * BRANCH_W, 3 * BRANCH_W + N_HEADS
    w_int = jnp.swapaxes(w_in, 1, 2)
    w1, w1t = _prep_weights(w_int, wtot)
    b_f_pad = jnp.pad(b_f, ((0, 0), (0, FORGET_PAD - N_HEADS)))
    bfr = b_f_pad.reshape(depth, 1, FORGET_PAD)
    bfc = jnp.broadcast_to(b_f_pad[:, :BF16_ROWS, None], (depth, BF16_ROWS, LANES))
    wa, wb, wc, wo = (w.astype(BF16) for w in (w_a_out, w_b_out, w_c_out, w_o))
    bst = jnp.swapaxes(b_s, 1, 2)
    wst = jnp.tile(w_s[:, :, :t_new, :t_new], (1, 1, ns // t_new, ns // t_new))
    bsr = jnp.tile(jnp.swapaxes(b_s[:, :, :t_new], 1, 2), (1, ns // t_new, 1))
    pool, page = cache_k.shape[1], cache_k.shape[2]
    n_pages = page_table.shape[1]
    cache_kt = jnp.transpose(cache_k, (0, 1, 3, 4, 2)).reshape(depth * pool, BRANCH_W, page)
    cache_vt = jnp.transpose(cache_v, (0, 1, 3, 4, 2)).reshape(depth * pool, BRANCH_W, page)
    cache_lft = jnp.swapaxes(cache_logf, 2, 3).reshape(depth * pool, N_HEADS, page)
    pages_latest_first = page_table[:, ::-1].reshape(-1)
    zp1 = jnp.concatenate([state_conv[:, :, 1:2], jnp.zeros((depth, nb, t_new - 1, BRANCH_W), F32)], axis=2)
    zp2 = jnp.concatenate([state_conv, jnp.zeros((depth, nb, t_new - 2, BRANCH_W), F32)], axis=2)
    zp1, zp2 = zp1.reshape(depth, ns, BRANCH_W), zp2.reshape(depth, ns, BRANCH_W)

    hp = _ln_rows(x_prompt.reshape(bsz * seq, d), ln_in_g, ln_in_b, tm).reshape(bsz, seq, d)
    hs = _ln_rows(x_sample.reshape(ns, d), ln_in_g, ln_in_b, ns)

    outs = {n: [] for n in ("lp", "cp", "ks", "vs", "ls", "cs", "us")}
    kv_all = (jnp.zeros((depth, bsz, seq, BRANCH_W), F32), jnp.zeros((depth, bsz, BRANCH_W, seq), F32))
    for l in range(depth):
        row1 = lambda a: a[l].reshape(1, -1)
        qa, ka, va, ko_all, vo_all, lf, tail, sga, sma, mbc = _prompt_proj(
            l, depth, kv_all, hp, w1, w1t, bfc[l], conv_w[l], row1(ln_v_g), row1(ln_v_b), w_s[l], bst[l],
            wb[l], wc[l], tm)
        kv_all = (ko_all, vo_all)
        ya = _flash(qa, ka, va, tq, tq)
        flat = lambda a: a.reshape(bsz * seq, a.shape[-1])
        hp = _merge(flat(hp), flat(ya), flat(sga), flat(sma), flat(mbc), wa[l], wo[l], row1(ln_g), row1(ln_b),
                    alpha, tm).reshape(bsz, seq, d)
        outs["lp"].append(lf); outs["cp"].append(tail)
        qs, ks, vs, lfs, cnt, zs, vns, sga, sma, mbc = _sample_proj(
            l, hs, w1, w1t, bfr[l], conv_w[l], row1(ln_v_g), row1(ln_v_b), wst[l], bsr[l], wb[l], wc[l],
            zp1[l], zp2[l], t_new)
        head_major = lambda a: jnp.pad(jnp.transpose(a.reshape(nb, t_new, N_HEADS, HEAD_DIM), (0, 2, 1, 3)),
                                       ((0, 0), (0, 0), (0, SUBLANES - t_new), (0, 0)))
        cn = jnp.pad(jnp.transpose(cnt.reshape(N_HEADS, nb, t_new), (1, 0, 2)), ((0, 0), (0, 0), (0, LANES - t_new)))
        yas = _sample_attn(pages_latest_first + l * pool, head_major(qs), head_major(ks), head_major(vs),
                           cn, cache_kt, cache_vt, cache_lft, n_pages, t_new)
        yas = jnp.transpose(yas[:, :, :t_new], (0, 2, 1, 3)).reshape(ns, BRANCH_W)
        hs = _merge(hs, yas.astype(BF16), sga, sma, mbc, wa[l], wo[l], row1(ln_g), row1(ln_b), alpha, ns)
        outs["ks"].append(ks); outs["vs"].append(vs); outs["ls"].append(lfs)
        outs["cs"].append(zs.reshape(nb, t_new, BRANCH_W)[:, t_new - (CONV_WIDTH - 1):])
        outs["us"].append(vns)

    st = {n: jnp.stack(v) for n, v in outs.items()}
    ko_all, vo_all = kv_all
    new_v_prompt = jnp.transpose(vo_all.reshape(depth, bsz, N_HEADS, HEAD_DIM, seq), (0, 1, 4, 2, 3))
    return (hp, hs.reshape(nb, t_new, d),
            ko_all.reshape(depth, bsz, seq, N_HEADS, HEAD_DIM), new_v_prompt,
            jnp.swapaxes(st["lp"], 2, 3), st["cp"],
            st["ks"].reshape(depth, nb, t_new, N_HEADS, HEAD_DIM), st["vs"].reshape(depth, nb, t_new, N_HEADS, HEAD_DIM),
            st["ls"].reshape(depth, nb, t_new, N_HEADS), st["cs"], st["us"].reshape(depth, nb, t_new, BRANCH_W))
```

```python
import functools
import math

import numpy as np
import jax
import jax.numpy as jnp
from jax import lax
from jax.experimental import pallas as pl
from jax.experimental.pallas import tpu as pltpu

F32 = jnp.float32
BF16 = jnp.bfloat16

N_HEADS = 8
HEAD_DIM = 64
BRANCH_W = 512
CHUNK = 128
N_GROUPS = 4
GROUP_DIM = BRANCH_W // N_GROUPS
CONV_WIDTH = 3
LN_EPS = 1e-5
LANES = 128
SUBLANES = 8
BF16_ROWS = 16
FORGET_PAD = LANES
ROW_SLOTS = ("k", "ga", "cb", "cc", "ch", "gb", "u", "vc", "gc")
W1_ALIGN = 2048
W1_CHUNK = 256
W1T_ROWS = 2 * BRANCH_W + W1_CHUNK
SCALE = HEAD_DIM ** -0.5
LOG2E = math.log2(math.e)
VMEM_LIMIT = 56 * 1024 * 1024
PAGES_PER_STEP = 16

AUG0 = HEAD_DIM
V_ROWS = LANES


def _offsets(d_model):
    off = {n: i * BRANCH_W for i, n in enumerate(ROW_SLOTS)}
    base = len(ROW_SLOTS) * BRANCH_W
    off["ma"], off["mb"], off["mc"] = base, base + d_model, base + 2 * d_model
    used = base + 3 * d_model
    return off, -(-used // W1_ALIGN) * W1_ALIGN


def _dot(a, b):
    return jnp.dot(a, b, preferred_element_type=F32)


def _dot_nt(a, b):
    return lax.dot_general(a, b, (((1,), (1,)), ((), ())), preferred_element_type=F32)


def _split3(x):
    hi = x.astype(BF16)
    r = x - hi.astype(F32)
    mid = r.astype(BF16)
    lo = (r - mid.astype(F32)).astype(BF16)
    return hi, mid, lo


def _sigmoid(x):
    return 1.0 / (1.0 + jnp.exp(-x))


def _silu(x):
    return x * _sigmoid(x)


def _log_sigmoid(x):
    return jnp.minimum(x, 0.0) - jnp.log1p(jnp.exp(-jnp.abs(x)))


def _layer_norm(x, g, b):
    mu = jnp.mean(x, axis=-1, keepdims=True)
    xc = x - mu
    var = jnp.mean(xc * xc, axis=-1, keepdims=True)
    return xc * lax.rsqrt(var + LN_EPS) * g + b


def _head_tiles(x):
    out = []
    for j in range(BRANCH_W // LANES):
        blk = x[:, j * LANES:(j + 1) * LANES]
        out.append(blk)
        out.append(pltpu.roll(blk, HEAD_DIM, axis=1))
    return out


def _ln_kernel(x_ref, g_ref, b_ref, o_ref):
    o_ref[...] = _layer_norm(x_ref[...], g_ref[...], b_ref[...])


def _ln_rows(x, g, b, tm):
    n, d = x.shape
    return pl.pallas_call(
        _ln_kernel,
        grid=(n // tm,),
        in_specs=[pl.BlockSpec((tm, d), lambda i: (i, 0)),
                  pl.BlockSpec((1, d), lambda i: (0, 0)),
                  pl.BlockSpec((1, d), lambda i: (0, 0))],
        out_specs=pl.BlockSpec((tm, d), lambda i: (i, 0)),
        out_shape=jax.ShapeDtypeStruct((n, d), F32),
        name="ln_in",
    )(x, g.reshape(1, d), b.reshape(1, d))


def _w1_kernel(a_ref, b_ref, o_ref, *, n_k, n_real, shift):
    j = pl.program_id(1)

    @pl.when(j < n_k)
    def _():
        o_ref[0] = a_ref[0].T.astype(BF16)

    @pl.when((j >= n_k) & (j < n_real))
    def _():
        o_ref[0] = jnp.concatenate([a_ref[0, shift:, :], b_ref[0, :shift, :]], axis=0).T.astype(BF16)

    @pl.when(j >= n_real)
    def _():
        o_ref[0] = jnp.zeros(o_ref.shape[1:], BF16)


def _w1t_kernel(a_ref, o_ref, *, n_chunks):
    c = pl.program_id(1)
    x = a_ref[0]
    keep = (c < n_chunks - 1) | (lax.broadcasted_iota(jnp.int32, x.shape, 0) < N_HEADS)
    o_ref[0] = jnp.where(keep, x, 0.0).astype(BF16)


def _prep_weights(w_int, wtot):
    depth, p_total, d = w_int.shape
    k0, v0, f0 = BRANCH_W, 2 * BRANCH_W, 3 * BRANCH_W
    g0 = f0 + N_HEADS
    n_k = BRANCH_W // W1_CHUNK
    n_real = n_k + (p_total - g0) // W1_CHUNK
    shift = g0 % W1_CHUNK
    assert (p_total - g0) % W1_CHUNK == 0 and wtot % W1_CHUNK == 0 and 0 < shift and shift % SUBLANES == 0
    last_blk = (p_total - 1) // W1_CHUNK

    def blk_a(j):
        jj = jnp.minimum(j, n_real - 1)
        return jnp.where(jj < n_k, jj + k0 // W1_CHUNK, jj - n_k + g0 // W1_CHUNK)

    w1 = pl.pallas_call(
        functools.partial(_w1_kernel, n_k=n_k, n_real=n_real, shift=shift),
        grid=(depth, wtot // W1_CHUNK),
        in_specs=[pl.BlockSpec((1, W1_CHUNK, d), lambda l, j: (l, blk_a(j), 0)),
                  pl.BlockSpec((1, W1_CHUNK, d), lambda l, j: (l, jnp.minimum(blk_a(j) + 1, last_blk), 0))],
        out_specs=pl.BlockSpec((1, d, W1_CHUNK), lambda l, j: (l, 0, j)),
        out_shape=jax.ShapeDtypeStruct((depth, d, wtot), BF16),
        name="w1_rowmajor",
    )(w_int, w_int)
    n_chunks = W1T_ROWS // W1_CHUNK
    n_q = BRANCH_W // W1_CHUNK
    w1t = pl.pallas_call(
        functools.partial(_w1t_kernel, n_chunks=n_chunks),
        grid=(depth, n_chunks),
        in_specs=[pl.BlockSpec((1, W1_CHUNK, d),
                               lambda l, c: (l, jnp.where(c < n_q, c, c - n_q + v0 // W1_CHUNK), 0))],
        out_specs=pl.BlockSpec((1, W1_CHUNK, d), lambda l, c: (l, c, 0)),
        out_shape=jax.ShapeDtypeStruct((depth, W1T_ROWS, d), BF16),
        name="w1_transposed",
    )(w_int)
    return w1, w1t


def _conv_gate(z, z1, z2, cb, gb, cw_ref):
    y = cw_ref[0:1, :] * z2 + cw_ref[1:2, :] * z1 + cw_ref[2:3, :] * z
    return (cb * y * _silu(gb)).astype(BF16)


def _prompt_proj_kernel(layer_ref, *refs, tm, d_model, n_in, ln_in):
    del layer_ref
    (h_ref, w1_ref, w1t_ref, bfc_ref, cw_ref, lvg_ref, lvb_ref, ws_ref, bst_ref, wb_ref, wc_ref) = refs[:11]
    (qa_ref, ka_ref, va_ref, ko_ref, vo_ref, lf_ref, tail_ref, sga_ref, sma_ref, mbc_ref) = refs[n_in:n_in + 10]
    ccol_ref, zbuf_ref = refs[-2:]
    off, _ = _offsets(d_model)
    i = pl.program_id(1)

    @pl.when(i == 0)
    def _():
        ccol_ref[...] = jnp.zeros_like(ccol_ref)
        zbuf_ref[0:SUBLANES, :] = jnp.zeros((SUBLANES, BRANCH_W), F32)

    if ln_in:
        lng_ref, lnb_ref, hn_ref = refs[11], refs[12], refs[n_in + 10]
        hn = _layer_norm(h_ref[0], lng_ref[...], lnb_ref[...])
        hn_ref[0] = hn
        hb = hn.astype(BF16)
    else:
        hb = h_ref[0].astype(BF16)

    def proj(name, width=BRANCH_W):
        return _dot(hb, w1_ref[:, off[name]:off[name] + width])

    pt = _dot_nt(w1t_ref[0:2 * BRANCH_W + BF16_ROWS, :], hb)

    logf_c = _log_sigmoid(pt[2 * BRANCH_W:2 * BRANCH_W + BF16_ROWS] + bfc_ref[:, 0:1])
    lf_ref[0] = logf_c[0:N_HEADS]
    upper = jnp.where(lax.broadcasted_iota(jnp.int32, (tm, tm), 0) <= lax.broadcasted_iota(jnp.int32, (tm, tm), 1),
                      1.0, 0.0).astype(BF16)
    c_c = ccol_ref[:, 0:1]
    for piece in _split3(logf_c):
        c_c = c_c + _dot(piece, upper)
    ccol_ref[...] = jnp.broadcast_to(c_c[:, tm - 1:tm], ccol_ref.shape)
    c_c = c_c * LOG2E
    c3c = [p.astype(F32) for p in _split3(c_c)]
    c_r = jnp.concatenate([c_c, jnp.zeros((LANES - BF16_ROWS, tm), F32)], axis=0).T
    c3r = [p.astype(F32) for p in _split3(c_r)]

    k = proj("k")
    ko_ref[0, 0] = k
    vo_ref[0, 0] = pt[BRANCH_W:2 * BRANCH_W]
    kh = _head_tiles(k)
    lane = lax.broadcasted_iota(jnp.int32, (tm, LANES), 1)
    row8 = lax.broadcasted_iota(jnp.int32, (SUBLANES, tm), 0)
    zpad = jnp.zeros((LANES - HEAD_DIM - SUBLANES, tm), F32)
    vpad = jnp.zeros((V_ROWS - HEAD_DIM - SUBLANES, tm), F32)
    ones_row = jnp.where(row8 == 0, 1.0, 0.0)
    for h in range(N_HEADS):
        cb3 = [jnp.broadcast_to(p[:, h:h + 1], (tm, LANES)) for p in c3r]
        ak = jnp.where(lane < AUG0 + 3, 1.0, jnp.where(lane == AUG0 + 3, -cb3[0], jnp.where(
            lane == AUG0 + 4, -cb3[1], jnp.where(lane == AUG0 + 5, -cb3[2], 0.0))))
        ka_ref[0, h] = jnp.where(lane < HEAD_DIM, kh[h], ak).astype(BF16)
        aq = jnp.where(row8 == 0, c3c[0][h:h + 1], jnp.where(row8 == 1, c3c[1][h:h + 1], jnp.where(
            row8 == 2, c3c[2][h:h + 1], jnp.where(row8 < 6, 1.0, 0.0))))
        qt = pt[h * HEAD_DIM:(h + 1) * HEAD_DIM] * (SCALE * LOG2E)
        qa_ref[0, h] = jnp.concatenate([qt, aq, zpad], axis=0).astype(BF16)
        vt = pt[BRANCH_W + h * HEAD_DIM:BRANCH_W + (h + 1) * HEAD_DIM]
        va_ref[0, h] = jnp.concatenate([vt, ones_row, vpad], axis=0).astype(BF16)

    sga_ref[0] = _silu(proj("ga")).astype(BF16)

    z = proj("cc") * proj("ch")
    zbuf_ref[SUBLANES:SUBLANES + tm, :] = z
    z1 = zbuf_ref[SUBLANES - 1:SUBLANES - 1 + tm, :]
    z2 = zbuf_ref[SUBLANES - 2:SUBLANES - 2 + tm, :]
    zbuf_ref[0:SUBLANES, :] = z[tm - SUBLANES:tm, :]
    tail_ref[0] = z[tm - (CONV_WIDTH - 1):tm, :]
    ob = _dot(_conv_gate(z, z1, z2, proj("cb"), proj("gb"), cw_ref), wb_ref[...])

    vn = _layer_norm(proj("vc"), lvg_ref[...], lvb_ref[...]).astype(BF16)
    tril = lax.broadcasted_iota(jnp.int32, (CHUNK, CHUNK), 0) >= lax.broadcasted_iota(jnp.int32, (CHUNK, CHUNK), 1)
    wm = [jnp.where(tril, ws_ref[g], 0.0).astype(BF16) for g in range(N_GROUPS)]
    rows = []
    for cidx in range(tm // CHUNK):
        rows.append(jnp.concatenate(
            [_dot(wm[g], vn[cidx * CHUNK:(cidx + 1) * CHUNK, g * GROUP_DIM:(g + 1) * GROUP_DIM]) + bst_ref[:, g:g + 1]
             for g in range(N_GROUPS)], axis=1))
    s = jnp.concatenate(rows, axis=0)
    oc = _dot((proj("u") * s * _silu(proj("gc"))).astype(BF16), wc_ref[...])

    sma_ref[0] = _sigmoid(proj("ma", d_model)).astype(BF16)
    mbc_ref[0] = (_sigmoid(proj("mb", d_model)) * ob + _sigmoid(proj("mc", d_model)) * oc).astype(BF16)


def _prompt_proj(layer, depth, kv_all, h, w1, w1t, bfc, cw, lvg, lvb, ws, bst, wb, wc, tm, ln_in=None):
    b, s, d = h.shape
    wtot = w1.shape[2]
    const = lambda shape: pl.BlockSpec(shape, lambda bi, i, lr: (0,) * len(shape), pipeline_mode=pl.Buffered(1))
    per_layer = lambda shape: pl.BlockSpec((None,) + shape, lambda bi, i, lr: (lr[0],) + (0,) * len(shape),
                                           pipeline_mode=pl.Buffered(1))
    rows = lambda w: pl.BlockSpec((1, tm, w), lambda bi, i, lr: (bi, i, 0))
    cols = lambda r: pl.BlockSpec((1, r, tm), lambda bi, i, lr: (bi, 0, i))
    heads_r = pl.BlockSpec((1, N_HEADS, tm, LANES), lambda bi, i, lr: (bi, 0, i, 0))
    heads_c = lambda r: pl.BlockSpec((1, N_HEADS, r, tm), lambda bi, i, lr: (bi, 0, 0, i))
    in_specs = [rows(d), per_layer((d, wtot)), per_layer((W1T_ROWS, d)),
                const((BF16_ROWS, LANES)), const((CONV_WIDTH, BRANCH_W)),
                const((1, BRANCH_W)), const((1, BRANCH_W)), const((N_GROUPS, CHUNK, CHUNK)),
                const((CHUNK, N_GROUPS)), const((BRANCH_W, d)), const((BRANCH_W, d))]
    args = [h, w1, w1t, bfc, cw, lvg, lvb, ws, bst, wb, wc]
    extra_out_specs, extra_out_shapes = [], []
    if ln_in is not None:
        in_specs += [const((1, d)), const((1, d))]
        args += [a.reshape(1, d) for a in ln_in]
        extra_out_specs, extra_out_shapes = [rows(d)], [jax.ShapeDtypeStruct((b, s, d), F32)]
    n_in = 1 + len(args)
    in_specs += [pl.BlockSpec(memory_space=pl.ANY)] * 2
    args += list(kv_all)
    aliases = {n_in: 3, n_in + 1: 4}
    grid_spec = pltpu.PrefetchScalarGridSpec(
        num_scalar_prefetch=1,
        grid=(b, s // tm),
        in_specs=in_specs,
        out_specs=[heads_c(LANES), heads_r, heads_c(V_ROWS),
                   pl.BlockSpec((1, 1, tm, BRANCH_W), lambda bi, i, lr: (lr[0], bi, i, 0)),
                   pl.BlockSpec((1, 1, BRANCH_W, tm), lambda bi, i, lr: (lr[0], bi, 0, i)),
                   cols(N_HEADS),
                   pl.BlockSpec((1, CONV_WIDTH - 1, BRANCH_W), lambda bi, i, lr: (bi, 0, 0)),
                   rows(BRANCH_W), rows(d), rows(d)] + extra_out_specs,
        scratch_shapes=[pltpu.VMEM((BF16_ROWS, LANES), F32), pltpu.VMEM((tm + SUBLANES, BRANCH_W), F32)],
    )
    return pl.pallas_call(
        functools.partial(_prompt_proj_kernel, tm=tm, d_model=d, n_in=len(args), ln_in=ln_in is not None),
        grid_spec=grid_spec,
        out_shape=[jax.ShapeDtypeStruct((b, N_HEADS, LANES, s), BF16), jax.ShapeDtypeStruct((b, N_HEADS, s, LANES), BF16),
                   jax.ShapeDtypeStruct((b, N_HEADS, V_ROWS, s), BF16),
                   jax.ShapeDtypeStruct((depth, b, s, BRANCH_W), F32), jax.ShapeDtypeStruct((depth, b, BRANCH_W, s), F32),
                   jax.ShapeDtypeStruct((b, N_HEADS, s), F32),
                   jax.ShapeDtypeStruct((b, CONV_WIDTH - 1, BRANCH_W), F32),
                   jax.ShapeDtypeStruct((b, s, BRANCH_W), BF16), jax.ShapeDtypeStruct((b, s, d), BF16),
                   jax.ShapeDtypeStruct((b, s, d), BF16)] + extra_out_shapes,
        input_output_aliases=aliases,
        compiler_params=pltpu.CompilerParams(dimension_semantics=("arbitrary", "arbitrary"),
                                             vmem_limit_bytes=VMEM_LIMIT),
        name="prompt_proj",
    )(jnp.full((1,), layer, jnp.int32), *args)


def _flash_kernel(qi_ref, ki_ref, qt_ref, k_ref, vt_ref, o_ref, m_ref, acc_ref, *, tq, tk, bsz):
    p = pl.program_id(0)
    qi = qi_ref[p]
    ki = ki_ref[p]
    n_units = bsz * N_HEADS

    @pl.when(ki == 0)
    def _():
        m_ref[...] = jnp.full_like(m_ref, -jnp.inf)
        acc_ref[...] = jnp.zeros_like(acc_ref)

    def body(masked):
        if masked:
            mask = (ki * tk + lax.broadcasted_iota(jnp.int32, (tk, tq), 0)
                    <= qi * tq + lax.broadcasted_iota(jnp.int32, (tk, tq), 1))

        def scores(u):
            b, h = divmod(u, N_HEADS)
            s = _dot(k_ref[b, h], qt_ref[b, h])
            return jnp.where(mask, s, -jnp.inf) if masked else s

        def softmax(u, s):
            m_old = m_ref[u]
            m_new = jnp.maximum(m_old, jnp.max(s, axis=0, keepdims=True))
            m_ref[u] = m_new
            return jnp.exp2(s - m_new).astype(BF16), jnp.exp2(m_old - m_new)

        def accumulate(u, pexp, a):
            b, h = divmod(u, N_HEADS)
            acc_ref[u] = a * acc_ref[u] + _dot(vt_ref[b, h], pexp)

        s_next = scores(0)
        prev = None
        for u in range(n_units):
            s_cur = s_next
            if u + 1 < n_units:
                s_next = scores(u + 1)
            cur = softmax(u, s_cur)
            if prev is not None:
                accumulate(u - 1, *prev)
            prev = cur
        accumulate(n_units - 1, *prev)

    full = (ki + 1) * tk - 1 <= qi * tq

    @pl.when(full)
    def _():
        body(False)

    @pl.when(jnp.logical_not(full))
    def _():
        body(True)

    @pl.when(ki == (qi * tq + tq - 1) // tk)
    def _():
        for b in range(bsz):
            for j in range(N_HEADS // 2):
                a0, a1 = acc_ref[b * N_HEADS + 2 * j], acc_ref[b * N_HEADS + 2 * j + 1]
                yy = jnp.concatenate([a0[:HEAD_DIM] * (1.0 / a0[AUG0:AUG0 + 1]),
                                      a1[:HEAD_DIM] * (1.0 / a1[AUG0:AUG0 + 1])], axis=0)
                o_ref[b, :, j * LANES:(j + 1) * LANES] = yy.T.astype(BF16)


def _flash(qa, ka, va, tq, tk):
    b, _, s, _ = ka.shape
    pairs = [(qi, ki) for qi in range(s // tq) for ki in range((qi * tq + tq - 1) // tk + 1)]
    qi_tab = jnp.asarray(np.array([p[0] for p in pairs], np.int32))
    ki_tab = jnp.asarray(np.array([p[1] for p in pairs], np.int32))
    grid_spec = pltpu.PrefetchScalarGridSpec(
        num_scalar_prefetch=2,
        grid=(len(pairs),),
        in_specs=[pl.BlockSpec((b, N_HEADS, LANES, tq), lambda p, qt, kt: (0, 0, 0, qt[p])),
                  pl.BlockSpec((b, N_HEADS, tk, LANES), lambda p, qt, kt: (0, 0, kt[p], 0)),
                  pl.BlockSpec((b, N_HEADS, V_ROWS, tk), lambda p, qt, kt: (0, 0, 0, kt[p]))],
        out_specs=pl.BlockSpec((b, tq, BRANCH_W), lambda p, qt, kt: (0, qt[p], 0)),
        scratch_shapes=[pltpu.VMEM((b * N_HEADS, 1, tq), F32), pltpu.VMEM((b * N_HEADS, V_ROWS, tq), F32)],
    )
    return pl.pallas_call(
        functools.partial(_flash_kernel, tq=tq, tk=tk, bsz=b),
        grid_spec=grid_spec,
        out_shape=jax.ShapeDtypeStruct((b, s, BRANCH_W), BF16),
        compiler_params=pltpu.CompilerParams(dimension_semantics=("arbitrary",), vmem_limit_bytes=VMEM_LIMIT),
        name="prompt_flash",
    )(qi_tab, ki_tab, qa, ka, va)


def _merge_kernel(h_ref, ya_ref, sga_ref, sma_ref, mbc_ref, wa_ref, wo_ref, g_ref, b_ref, o_ref, *, alpha):
    oa = _dot((ya_ref[...].astype(F32) * sga_ref[...].astype(F32)).astype(BF16), wa_ref[...])
    m = sma_ref[...].astype(F32) * oa + mbc_ref[...].astype(F32)
    x = alpha * h_ref[...] + _dot(m.astype(BF16), wo_ref[...])
    o_ref[...] = _layer_norm(x, g_ref[...], b_ref[...])


def _merge(h, ya, sga, sma, mbc, wa, wo, g, b, alpha, tm):
    n, d = h.shape
    rows = lambda w: pl.BlockSpec((tm, w), lambda i: (i, 0))
    const = lambda shape: pl.BlockSpec(shape, lambda i: (0,) * len(shape))
    return pl.pallas_call(
        functools.partial(_merge_kernel, alpha=alpha),
        grid=(n // tm,),
        in_specs=[rows(d), rows(BRANCH_W), rows(BRANCH_W), rows(d), rows(d),
                  const((BRANCH_W, d)), const((d, d)), const((1, d)), const((1, d))],
        out_specs=rows(d),
        out_shape=jax.ShapeDtypeStruct((n, d), F32),
        compiler_params=pltpu.CompilerParams(dimension_semantics=("arbitrary",), vmem_limit_bytes=VMEM_LIMIT),
        name="merge",
    )(h, ya, sga, sma, mbc, wa, wo, g, b)


def _sample_proj_kernel(h_ref, w1_ref, w1t_ref, bf_ref, cw_ref, lvg_ref, lvb_ref, wst_ref, bsr_ref, wb_ref, wc_ref,
                        zp1_ref, zp2_ref,
                        q_ref, k_ref, v_ref, lf_ref, cnt_ref, z_ref, vn_ref, sga_ref, sma_ref, mbc_ref,
                        p_ref, zbuf_ref, *, tn, n_steps, t_new, d_model):
    off, _ = _offsets(d_model)
    j = pl.program_id(0)
    ns = h_ref.shape[0]
    hb = h_ref[...].astype(BF16)
    col = pl.multiple_of(j * tn, LANES)
    p_ref[:, pl.ds(col, tn)] = _dot(hb, w1_ref[...])

    @pl.when(j == n_steps - 1)
    def _():
        def proj(name, width=BRANCH_W):
            return p_ref[:, off[name]:off[name] + width]

        qv = _dot_nt(hb, w1t_ref[...])
        q_ref[...] = qv[:, 0:BRANCH_W]
        v_ref[...] = qv[:, BRANCH_W:2 * BRANCH_W]
        k_ref[...] = proj("k")
        logf = _log_sigmoid(qv[:, 2 * BRANCH_W:2 * BRANCH_W + FORGET_PAD] + bf_ref[...])
        lf_ref[...] = logf[:, :N_HEADS]
        lt = logf.T
        tpos = lax.broadcasted_iota(jnp.int32, lt.shape, 1) % t_new
        cn = lt
        shift = 1
        while shift < t_new:
            cn = cn + jnp.where(tpos >= shift, pltpu.roll(cn, shift, axis=1), 0.0)
            shift *= 2
        cnt_ref[...] = cn[0:N_HEADS, :]

        sga_ref[...] = _silu(proj("ga")).astype(BF16)

        z = proj("cc") * proj("ch")
        z_ref[...] = z
        zbuf_ref[0:SUBLANES, :] = jnp.zeros((SUBLANES, BRANCH_W), F32)
        zbuf_ref[SUBLANES:SUBLANES + ns, :] = z
        trow = lax.broadcasted_iota(jnp.int32, (ns, BRANCH_W), 0) % t_new
        z1 = jnp.where(trow < 1, zp1_ref[...], zbuf_ref[SUBLANES - 1:SUBLANES - 1 + ns, :])
        z2 = jnp.where(trow < 2, zp2_ref[...], zbuf_ref[SUBLANES - 2:SUBLANES - 2 + ns, :])
        ob = _dot(_conv_gate(z, z1, z2, proj("cb"), proj("gb"), cw_ref), wb_ref[...])

        vn = _layer_norm(proj("vc"), lvg_ref[...], lvb_ref[...])
        vn_ref[...] = vn
        vnb = vn.astype(BF16)
        r = lax.broadcasted_iota(jnp.int32, (ns, ns), 0)
        c = lax.broadcasted_iota(jnp.int32, (ns, ns), 1)
        keep = (r // t_new == c // t_new) & (c % t_new <= r % t_new)
        s = jnp.concatenate(
            [_dot(jnp.where(keep, wst_ref[g], 0.0).astype(BF16), vnb[:, g * GROUP_DIM:(g + 1) * GROUP_DIM])
             + bsr_ref[:, g:g + 1] for g in range(N_GROUPS)], axis=1)
        oc = _dot((proj("u") * s * _silu(proj("gc"))).astype(BF16), wc_ref[...])

        sma_ref[...] = _sigmoid(proj("ma", d_model)).astype(BF16)
        mbc_ref[...] = (_sigmoid(proj("mb", d_model)) * ob + _sigmoid(proj("mc", d_model)) * oc).astype(BF16)


def _sample_proj(layer, h, w1, w1t, bf, cw, lvg, lvb, wst, bsr, wb, wc, zp1, zp2, t_new):
    ns, d = h.shape
    wtot = w1.shape[2]
    tn = W1_ALIGN
    n_steps = wtot // tn
    const = lambda shape: pl.BlockSpec(shape, lambda j: (0,) * len(shape))
    f32 = lambda w: jax.ShapeDtypeStruct((ns, w), F32)
    bf16 = lambda w: jax.ShapeDtypeStruct((ns, w), BF16)
    return pl.pallas_call(
        functools.partial(_sample_proj_kernel, tn=tn, n_steps=n_steps, t_new=t_new, d_model=d),
        grid=(n_steps,),
        in_specs=[const((ns, d)), pl.BlockSpec((None, d, tn), lambda j: (layer, 0, j)),
                  pl.BlockSpec((None, W1T_ROWS, d), lambda j: (layer, 0, 0)), const((1, FORGET_PAD)),
                  const((CONV_WIDTH, BRANCH_W)), const((1, BRANCH_W)), const((1, BRANCH_W)),
                  const((N_GROUPS, ns, ns)), const((ns, N_GROUPS)), const((BRANCH_W, d)), const((BRANCH_W, d)),
                  const((ns, BRANCH_W)), const((ns, BRANCH_W))],
        out_specs=[const((ns, BRANCH_W)), const((ns, BRANCH_W)), const((ns, BRANCH_W)), const((ns, N_HEADS)),
                   const((N_HEADS, ns)), const((ns, BRANCH_W)), const((ns, BRANCH_W)),
                   const((ns, BRANCH_W)), const((ns, d)), const((ns, d))],
        out_shape=[f32(BRANCH_W), f32(BRANCH_W), f32(BRANCH_W), f32(N_HEADS),
                   jax.ShapeDtypeStruct((N_HEADS, ns), F32), f32(BRANCH_W), f32(BRANCH_W),
                   bf16(BRANCH_W), bf16(d), bf16(d)],
        scratch_shapes=[pltpu.VMEM((ns, wtot), F32), pltpu.VMEM((ns + SUBLANES, BRANCH_W), F32)],
        compiler_params=pltpu.CompilerParams(dimension_semantics=("arbitrary",), vmem_limit_bytes=VMEM_LIMIT),
        name="sample_proj",
    )(h, w1, w1t, bf, cw, lvg, lvb, wst, bsr, wb, wc, zp1, zp2)


def _sample_attn_kernel(rows_ref, q_ref, kn_ref, vn_ref, cn_ref, ck_hbm, cv_hbm, clf_hbm, o_ref,
                        kbuf, vbuf, lbuf, sem, m_ref, l_ref, acc_ref, car_ref, *, pp, t_new):
    step = pl.program_id(1)
    n_steps = pl.num_programs(1)
    g = pl.program_id(0) * n_steps + step
    total = pl.num_programs(0) * n_steps
    slot = g % 2
    tpad = q_ref.shape[2]

    def page_copies(gg, sl):
        out = []
        for i in range(pp):
            row = rows_ref[gg * pp + i]
            out.append(pltpu.make_async_copy(ck_hbm.at[row], kbuf.at[sl, i], sem.at[sl, 0]))
            out.append(pltpu.make_async_copy(cv_hbm.at[row], vbuf.at[sl, i], sem.at[sl, 1]))
            out.append(pltpu.make_async_copy(clf_hbm.at[row], lbuf.at[sl, i], sem.at[sl, 2]))
        return out

    @pl.when(g == 0)
    def _():
        for c in page_copies(0, 0):
            c.start()

    @pl.when(g + 1 < total)
    def _():
        for c in page_copies(g + 1, 1 - slot):
            c.start()

    @pl.when(step == 0)
    def _():
        car_ref[...] = jnp.zeros_like(car_ref)
        q3 = q_ref[0] * SCALE
        trow = lax.broadcasted_iota(jnp.int32, (N_HEADS, tpad, 1), 1)
        ss = []
        for t in range(t_new):
            st = jnp.sum(q3 * kn_ref[0, :, t:t + 1, :], axis=2, keepdims=True) - cn_ref[0, :, t:t + 1][:, :, None]
            ss.append(jnp.where(trow >= t, st, -jnp.inf))
        m = ss[0]
        for st in ss[1:]:
            m = jnp.maximum(m, st)
        l = jnp.zeros((N_HEADS, tpad, 1), F32)
        acc = jnp.zeros((N_HEADS, tpad, HEAD_DIM), F32)
        for t in range(t_new):
            pt = jnp.exp(ss[t] - m)
            l = l + pt
            acc = acc + pt * vn_ref[0, :, t:t + 1, :]
        m_ref[...] = m
        l_ref[...] = l
        acc_ref[...] = acc

    for c in page_copies(g, slot):
        c.wait()

    ci = lax.broadcasted_iota(jnp.int32, (LANES, 2 * LANES), 0)
    cj = lax.broadcasted_iota(jnp.int32, (LANES, 2 * LANES), 1)
    usum = jnp.where((cj >= LANES) | (ci > cj), 1.0, 0.0).astype(BF16)
    carry = car_ref[...]
    bias = []
    for i in range(pp):
        r = jnp.zeros((N_HEADS, 2 * LANES), F32)
        for piece in _split3(lbuf[slot, i]):
            r = r + _dot(piece, usum)
        bias.append(r[:, :LANES] + carry)
        carry = carry + r[:, LANES:]
    car_ref[...] = carry
    bias = jnp.concatenate(bias, axis=1)

    ss = []
    for h in range(N_HEADS):
        qh = (q_ref[0, h] * SCALE).astype(BF16)
        kt = jnp.concatenate([kbuf[slot, i, h * HEAD_DIM:(h + 1) * HEAD_DIM, :] for i in range(pp)],
                             axis=1).astype(BF16)
        ss.append(_dot(qh, kt) + bias[h:h + 1, :])
    ps, al = [], []
    for h in range(N_HEADS):
        m_old = m_ref[h]
        m_new = jnp.maximum(m_old, jnp.max(ss[h], axis=1, keepdims=True))
        a = jnp.exp(m_old - m_new)
        p = jnp.exp(ss[h] - m_new)
        l_ref[h] = a * l_ref[h] + jnp.sum(p, axis=1, keepdims=True)
        m_ref[h] = m_new
        ps.append(p.astype(BF16))
        al.append(a)
    for h in range(N_HEADS):
        vt = jnp.concatenate([vbuf[slot, i, h * HEAD_DIM:(h + 1) * HEAD_DIM, :] for i in range(pp)],
                             axis=1).astype(BF16)
        acc_ref[h] = al[h] * acc_ref[h] + _dot_nt(ps[h], vt)

    @pl.when(step == n_steps - 1)
    def _():
        for h in range(N_HEADS):
            o_ref[0, h] = acc_ref[h] * (1.0 / l_ref[h])


def _sample_attn(rows, q, kn, vn, cn, cache_kt, cache_vt, cache_lft, n_pages, t_new):
    nb, tpad = q.shape[0], q.shape[2]
    width, page = cache_kt.shape[1], cache_kt.shape[2]
    pp = PAGES_PER_STEP if n_pages % PAGES_PER_STEP == 0 else n_pages
    n_steps = n_pages // pp
    per_seq = lambda shape: pl.BlockSpec((1,) + shape, lambda b, s, rref: (b,) + (0,) * len(shape))
    in_hbm = pl.BlockSpec(memory_space=pl.ANY)
    grid_spec = pltpu.PrefetchScalarGridSpec(
        num_scalar_prefetch=1,
        grid=(nb, n_steps),
        in_specs=[per_seq((N_HEADS, tpad, HEAD_DIM))] * 3 + [per_seq((N_HEADS, LANES))] + [in_hbm] * 3,
        out_specs=per_seq((N_HEADS, tpad, HEAD_DIM)),
        scratch_shapes=[pltpu.VMEM((2, pp, width, page), F32), pltpu.VMEM((2, pp, width, page), F32),
                        pltpu.VMEM((2, pp, N_HEADS, page), F32), pltpu.SemaphoreType.DMA((2, 3)),
                        pltpu.VMEM((N_HEADS, tpad, 1), F32), pltpu.VMEM((N_HEADS, tpad, 1), F32),
                        pltpu.VMEM((N_HEADS, tpad, HEAD_DIM), F32), pltpu.VMEM((N_HEADS, LANES), F32)],
    )
    return pl.pallas_call(
        functools.partial(_sample_attn_kernel, pp=pp, t_new=t_new),
        grid_spec=grid_spec,
        out_shape=jax.ShapeDtypeStruct((nb, N_HEADS, tpad, HEAD_DIM), F32),
        compiler_params=pltpu.CompilerParams(dimension_semantics=("arbitrary", "arbitrary"),
                                             vmem_limit_bytes=VMEM_LIMIT),
        name="sample_attn",
    )(rows, q, kn, vn, cn, cache_kt, cache_vt, cache_lft)


def _pick_tile(n, pref):
    t = pref
    while n % t:
        t //= 2
    return t


def kernel(x_prompt, x_sample, cache_k, cache_v, cache_logf, state_conv, page_table, ln_in_g, ln_in_b, w_in, b_f,
           conv_w, ln_v_g, ln_v_b, w_s, b_s, w_a_out, w_b_out, w_c_out, w_o, ln_g, ln_b):
    depth, d = w_in.shape[0], w_in.shape[1]
    bsz, seq, _ = x_prompt.shape
    nb, t_new, _ = x_sample.shape
    ns = nb * t_new
    assert seq % CHUNK == 0 and t_new <= SUBLANES and page_table.shape[0] == nb
    alpha = (2.0 * depth) ** 0.25
    tm = _pick_tile(seq, 512)
    tq = _pick_tile(seq, 512)

    _, wtot = _offsets(d)
    q0, k0, v0, f0, g0 = 0, BRANCH_W, 2 * BRANCH_W, 3 * BRANCH_W, 3 * BRANCH_W + N_HEADS
    w_int = jnp.swapaxes(w_in, 1, 2)
    w1, w1t = _prep_weights(w_int, wtot)
    b_f_pad = jnp.pad(b_f, ((0, 0), (0, FORGET_PAD - N_HEADS)))
    bfr = b_f_pad.reshape(depth, 1, FORGET_PAD)
    bfc = jnp.broadcast_to(b_f_pad[:, :BF16_ROWS, None], (depth, BF16_ROWS, LANES))
    wa, wb, wc, wo = (w.astype(BF16) for w in (w_a_out, w_b_out, w_c_out, w_o))
    bst = jnp.swapaxes(b_s, 1, 2)
    wst = jnp.tile(w_s[:, :, :t_new, :t_new], (1, 1, ns // t_new, ns // t_new))
    bsr = jnp.tile(jnp.swapaxes(b_s[:, :, :t_new], 1, 2), (1, ns // t_new, 1))
    pool, page = cache_k.shape[1], cache_k.shape[2]
    n_pages = page_table.shape[1]
    cache_kt = jnp.transpose(cache_k, (0, 1, 3, 4, 2)).reshape(depth * pool, BRANCH_W, page)
    cache_vt = jnp.transpose(cache_v, (0, 1, 3, 4, 2)).reshape(depth * pool, BRANCH_W, page)
    cache_lft = jnp.swapaxes(cache_logf, 2, 3).reshape(depth * pool, N_HEADS, page)
    pages_latest_first = page_table[:, ::-1].reshape(-1)
    zp1 = jnp.concatenate([state_conv[:, :, 1:2], jnp.zeros((depth, nb, t_new - 1, BRANCH_W), F32)], axis=2)
    zp2 = jnp.concatenate([state_conv, jnp.zeros((depth, nb, t_new - 2, BRANCH_W), F32)], axis=2)
    zp1, zp2 = zp1.reshape(depth, ns, BRANCH_W), zp2.reshape(depth, ns, BRANCH_W)

    hp = x_prompt
    hs = _ln_rows(x_sample.reshape(ns, d), ln_in_g, ln_in_b, ns)

    outs = {n: [] for n in ("lp", "cp", "ks", "vs", "ls", "cs", "us")}
    kv_all = (jnp.zeros((depth, bsz, seq, BRANCH_W), F32), jnp.zeros((depth, bsz, BRANCH_W, seq), F32))
    for l in range(depth):
        row1 = lambda a: a[l].reshape(1, -1)
        qa, ka, va, ko_all, vo_all, lf, tail, sga, sma, mbc, *hn = _prompt_proj(
            l, depth, kv_all, hp, w1, w1t, bfc[l], conv_w[l], row1(ln_v_g), row1(ln_v_b), w_s[l], bst[l],
            wb[l], wc[l], tm, ln_in=(ln_in_g, ln_in_b) if l == 0 else None)
        if hn:
            hp = hn[0]
        kv_all = (ko_all, vo_all)
        ya = _flash(qa, ka, va, tq, tq)
        flat = lambda a: a.reshape(bsz * seq, a.shape[-1])
        hp = _merge(flat(hp), flat(ya), flat(sga), flat(sma), flat(mbc), wa[l], wo[l], row1(ln_g), row1(ln_b),
                    alpha, tm).reshape(bsz, seq, d)
        outs["lp"].append(lf); outs["cp"].append(tail)
        qs, ks, vs, lfs, cnt, zs, vns, sga, sma, mbc = _sample_proj(
            l, hs, w1, w1t, bfr[l], conv_w[l], row1(ln_v_g), row1(ln_v_b), wst[l], bsr[l], wb[l], wc[l],
            zp1[l], zp2[l], t_new)
        head_major = lambda a: jnp.pad(jnp.transpose(a.reshape(nb, t_new, N_HEADS, HEAD_DIM), (0, 2, 1, 3)),
                                       ((0, 0), (0, 0), (0, SUBLANES - t_new), (0, 0)))
        cn = jnp.pad(jnp.transpose(cnt.reshape(N_HEADS, nb, t_new), (1, 0, 2)), ((0, 0), (0, 0), (0, LANES - t_new)))
        yas = _sample_attn(pages_latest_first + l * pool, head_major(qs), head_major(ks), head_major(vs),
                           cn, cache_kt, cache_vt, cache_lft, n_pages, t_new)
        yas = jnp.transpose(yas[:, :, :t_new], (0, 2, 1, 3)).reshape(ns, BRANCH_W)
        hs = _merge(hs, yas.astype(BF16), sga, sma, mbc, wa[l], wo[l], row1(ln_g), row1(ln_b), alpha, ns)
        outs["ks"].append(ks); outs["vs"].append(vs); outs["ls"].append(lfs)
        outs["cs"].append(zs.reshape(nb, t_new, BRANCH_W)[:, t_new - (CONV_WIDTH - 1):])
        outs["us"].append(vns)

    st = {n: jnp.stack(v) for n, v in outs.items()}
    ko_all, vo_all = kv_all
    new_v_prompt = jnp.transpose(vo_all.reshape(depth, bsz, N_HEADS, HEAD_DIM, seq), (0, 1, 4, 2, 3))
    return (hp, hs.reshape(nb, t_new, d),
            ko_all.reshape(depth, bsz, seq, N_HEADS, HEAD_DIM), new_v_prompt,
            jnp.swapaxes(st["lp"], 2, 3), st["cp"],
            st["ks"].reshape(depth, nb, t_new, N_HEADS, HEAD_DIM), st["vs"].reshape(depth, nb, t_new, N_HEADS, HEAD_DIM),
            st["ls"].reshape(depth, nb, t_new, N_HEADS), st["cs"], st["us"].reshape(depth, nb, t_new, BRANCH_W))
```

```python
import functools
import math

import numpy as np
import jax
import jax.numpy as jnp
from jax import lax
from jax.experimental import pallas as pl
from jax.experimental.pallas import tpu as pltpu

F32 = jnp.float32
BF16 = jnp.bfloat16

N_HEADS = 8
HEAD_DIM = 64
BRANCH_W = 512
CHUNK = 128
N_GROUPS = 4
GROUP_DIM = BRANCH_W // N_GROUPS
CONV_WIDTH = 3
LN_EPS = 1e-5
LANES = 128
SUBLANES = 8
BF16_ROWS = 16
FORGET_PAD = LANES
ROW_SLOTS = ("k", "ga", "cb", "cc", "ch", "gb", "u", "vc", "gc")
W1_ALIGN = 2048
W1_CHUNK = 256
W1T_ROWS = 2 * BRANCH_W + W1_CHUNK
SCALE = HEAD_DIM ** -0.5
LOG2E = math.log2(math.e)
VMEM_LIMIT = 56 * 1024 * 1024
PAGES_PER_STEP = 16

AUG0 = HEAD_DIM
V_ROWS = LANES


def _offsets(d_model):
    off = {n: i * BRANCH_W for i, n in enumerate(ROW_SLOTS)}
    base = len(ROW_SLOTS) * BRANCH_W
    off["ma"], off["mb"], off["mc"] = base, base + d_model, base + 2 * d_model
    used = base + 3 * d_model
    return off, -(-used // W1_ALIGN) * W1_ALIGN


def _dot(a, b):
    return jnp.dot(a, b, preferred_element_type=F32)


def _dot_nt(a, b):
    return lax.dot_general(a, b, (((1,), (1,)), ((), ())), preferred_element_type=F32)


def _split3(x):
    hi = x.astype(BF16)
    r = x - hi.astype(F32)
    mid = r.astype(BF16)
    lo = (r - mid.astype(F32)).astype(BF16)
    return hi, mid, lo


def _sigmoid(x):
    return 1.0 / (1.0 + jnp.exp(-x))


def _silu(x):
    return x * _sigmoid(x)


def _log_sigmoid(x):
    return jnp.minimum(x, 0.0) - jnp.log1p(jnp.exp(-jnp.abs(x)))


def _layer_norm(x, g, b):
    mu = jnp.mean(x, axis=-1, keepdims=True)
    xc = x - mu
    var = jnp.mean(xc * xc, axis=-1, keepdims=True)
    return xc * lax.rsqrt(var + LN_EPS) * g + b


def _head_tiles(x):
    out = []
    for j in range(BRANCH_W // LANES):
        blk = x[:, j * LANES:(j + 1) * LANES]
        out.append(blk)
        out.append(pltpu.roll(blk, HEAD_DIM, axis=1))
    return out


def _ln_kernel(x_ref, g_ref, b_ref, o_ref):
    o_ref[...] = _layer_norm(x_ref[...], g_ref[...], b_ref[...])


def _ln_rows(x, g, b, tm):
    n, d = x.shape
    return pl.pallas_call(
        _ln_kernel,
        grid=(n // tm,),
        in_specs=[pl.BlockSpec((tm, d), lambda i: (i, 0)),
                  pl.BlockSpec((1, d), lambda i: (0, 0)),
                  pl.BlockSpec((1, d), lambda i: (0, 0))],
        out_specs=pl.BlockSpec((tm, d), lambda i: (i, 0)),
        out_shape=jax.ShapeDtypeStruct((n, d), F32),
        name="ln_in",
    )(x, g.reshape(1, d), b.reshape(1, d))


def _w1_kernel(a_ref, b_ref, o_ref, *, n_k, n_real, shift):
    j = pl.program_id(1)

    @pl.when(j < n_k)
    def _():
        o_ref[0] = a_ref[0].T.astype(BF16)

    @pl.when((j >= n_k) & (j < n_real))
    def _():
        o_ref[0] = jnp.concatenate([a_ref[0, shift:, :], b_ref[0]], axis=0).T.astype(BF16)

    @pl.when(j >= n_real)
    def _():
        o_ref[0] = jnp.zeros(o_ref.shape[1:], BF16)


def _w1t_kernel(a_ref, o_ref, *, n_chunks):
    c = pl.program_id(1)
    x = a_ref[0]
    keep = (c < n_chunks - 1) | (lax.broadcasted_iota(jnp.int32, x.shape, 0) < N_HEADS)
    o_ref[0] = jnp.where(keep, x, 0.0).astype(BF16)


def _prep_weights(w_int, wtot):
    depth, p_total, d = w_int.shape
    k0, v0, f0 = BRANCH_W, 2 * BRANCH_W, 3 * BRANCH_W
    g0 = f0 + N_HEADS
    n_k = BRANCH_W // W1_CHUNK
    n_real = n_k + (p_total - g0) // W1_CHUNK
    shift = g0 % W1_CHUNK
    assert (p_total - g0) % W1_CHUNK == 0 and wtot % W1_CHUNK == 0 and 0 < shift and W1_CHUNK % shift == 0
    assert shift % SUBLANES == 0 and p_total % shift == 0
    per_chunk = W1_CHUNK // shift

    def blk_a(j):
        jj = jnp.minimum(j, n_real - 1)
        return jnp.where(jj < n_k, jj + k0 // W1_CHUNK, jj - n_k + g0 // W1_CHUNK)

    w1 = pl.pallas_call(
        functools.partial(_w1_kernel, n_k=n_k, n_real=n_real, shift=shift),
        grid=(depth, wtot // W1_CHUNK),
        in_specs=[pl.BlockSpec((1, W1_CHUNK, d), lambda l, j: (l, blk_a(j), 0)),
                  pl.BlockSpec((1, shift, d),
                               lambda l, j: (l, jnp.minimum((blk_a(j) + 1) * per_chunk, p_total // shift - 1), 0))],
        out_specs=pl.BlockSpec((1, d, W1_CHUNK), lambda l, j: (l, 0, j)),
        out_shape=jax.ShapeDtypeStruct((depth, d, wtot), BF16),
        name="w1_rowmajor",
    )(w_int, w_int)
    n_chunks = W1T_ROWS // W1_CHUNK
    n_q = BRANCH_W // W1_CHUNK
    w1t = pl.pallas_call(
        functools.partial(_w1t_kernel, n_chunks=n_chunks),
        grid=(depth, n_chunks),
        in_specs=[pl.BlockSpec((1, W1_CHUNK, d),
                               lambda l, c: (l, jnp.where(c < n_q, c, c - n_q + v0 // W1_CHUNK), 0))],
        out_specs=pl.BlockSpec((1, W1_CHUNK, d), lambda l, c: (l, c, 0)),
        out_shape=jax.ShapeDtypeStruct((depth, W1T_ROWS, d), BF16),
        name="w1_transposed",
    )(w_int)
    return w1, w1t


def _conv_gate(z, z1, z2, cb, gb, cw_ref):
    y = cw_ref[0:1, :] * z2 + cw_ref[1:2, :] * z1 + cw_ref[2:3, :] * z
    return (cb * y * _silu(gb)).astype(BF16)


def _prompt_proj_kernel(layer_ref, *refs, tm, d_model, n_in, ln_in):
    del layer_ref
    (h_ref, w1_ref, w1t_ref, bfc_ref, cw_ref, lvg_ref, lvb_ref, ws_ref, bst_ref, wb_ref, wc_ref) = refs[:11]
    (qa_ref, ka_ref, va_ref, ko_ref, vo_ref, lf_ref, tail_ref, sga_ref, sma_ref, mbc_ref) = refs[n_in:n_in + 10]
    ccol_ref, zbuf_ref = refs[-2:]
    off, _ = _offsets(d_model)
    i = pl.program_id(1)

    @pl.when(i == 0)
    def _():
        ccol_ref[...] = jnp.zeros_like(ccol_ref)
        zbuf_ref[0:SUBLANES, :] = jnp.zeros((SUBLANES, BRANCH_W), F32)

    if ln_in:
        lng_ref, lnb_ref, hn_ref = refs[11], refs[12], refs[n_in + 10]
        hn = _layer_norm(h_ref[0], lng_ref[...], lnb_ref[...])
        hn_ref[0] = hn
        hb = hn.astype(BF16)
    else:
        hb = h_ref[0].astype(BF16)

    def proj(name, width=BRANCH_W):
        return _dot(hb, w1_ref[:, off[name]:off[name] + width])

    pt = _dot_nt(w1t_ref[0:2 * BRANCH_W + BF16_ROWS, :], hb)

    logf_c = _log_sigmoid(pt[2 * BRANCH_W:2 * BRANCH_W + BF16_ROWS] + bfc_ref[:, 0:1])
    lf_ref[0] = logf_c[0:N_HEADS]
    upper = jnp.where(lax.broadcasted_iota(jnp.int32, (tm, tm), 0) <= lax.broadcasted_iota(jnp.int32, (tm, tm), 1),
                      1.0, 0.0).astype(BF16)
    c_c = ccol_ref[:, 0:1]
    for piece in _split3(logf_c):
        c_c = c_c + _dot(piece, upper)
    ccol_ref[...] = jnp.broadcast_to(c_c[:, tm - 1:tm], ccol_ref.shape)
    c_c = c_c * LOG2E
    c3c = [p.astype(F32) for p in _split3(c_c)]
    c_r = jnp.concatenate([c_c, jnp.zeros((LANES - BF16_ROWS, tm), F32)], axis=0).T
    c3r = [p.astype(F32) for p in _split3(c_r)]

    k = proj("k")
    ko_ref[0, 0] = k
    vo_ref[0, 0] = pt[BRANCH_W:2 * BRANCH_W]
    kh = _head_tiles(k)
    lane = lax.broadcasted_iota(jnp.int32, (tm, LANES), 1)
    row8 = lax.broadcasted_iota(jnp.int32, (SUBLANES, tm), 0)
    zpad = jnp.zeros((LANES - HEAD_DIM - SUBLANES, tm), F32)
    vpad = jnp.zeros((V_ROWS - HEAD_DIM - SUBLANES, tm), F32)
    ones_row = jnp.where(row8 == 0, 1.0, 0.0)
    for h in range(N_HEADS):
        cb3 = [jnp.broadcast_to(p[:, h:h + 1], (tm, LANES)) for p in c3r]
        ak = jnp.where(lane < AUG0 + 3, 1.0, jnp.where(lane == AUG0 + 3, -cb3[0], jnp.where(
            lane == AUG0 + 4, -cb3[1], jnp.where(lane == AUG0 + 5, -cb3[2], 0.0))))
        ka_ref[0, h] = jnp.where(lane < HEAD_DIM, kh[h], ak).astype(BF16)
        aq = jnp.where(row8 == 0, c3c[0][h:h + 1], jnp.where(row8 == 1, c3c[1][h:h + 1], jnp.where(
            row8 == 2, c3c[2][h:h + 1], jnp.where(row8 < 6, 1.0, 0.0))))
        qt = pt[h * HEAD_DIM:(h + 1) * HEAD_DIM] * (SCALE * LOG2E)
        qa_ref[0, h] = jnp.concatenate([qt, aq, zpad], axis=0).astype(BF16)
        vt = pt[BRANCH_W + h * HEAD_DIM:BRANCH_W + (h + 1) * HEAD_DIM]
        va_ref[0, h] = jnp.concatenate([vt, ones_row, vpad], axis=0).astype(BF16)

    sga_ref[0] = _silu(proj("ga")).astype(BF16)

    z = proj("cc") * proj("ch")
    zbuf_ref[SUBLANES:SUBLANES + tm, :] = z
    z1 = zbuf_ref[SUBLANES - 1:SUBLANES - 1 + tm, :]
    z2 = zbuf_ref[SUBLANES - 2:SUBLANES - 2 + tm, :]
    zbuf_ref[0:SUBLANES, :] = z[tm - SUBLANES:tm, :]
    tail_ref[0] = z[tm - (CONV_WIDTH - 1):tm, :]
    ob = _dot(_conv_gate(z, z1, z2, proj("cb"), proj("gb"), cw_ref), wb_ref[...])

    vn = _layer_norm(proj("vc"), lvg_ref[...], lvb_ref[...]).astype(BF16)
    tril = lax.broadcasted_iota(jnp.int32, (CHUNK, CHUNK), 0) >= lax.broadcasted_iota(jnp.int32, (CHUNK, CHUNK), 1)
    wm = [jnp.where(tril, ws_ref[g], 0.0).astype(BF16) for g in range(N_GROUPS)]
    rows = []
    for cidx in range(tm // CHUNK):
        rows.append(jnp.concatenate(
            [_dot(wm[g], vn[cidx * CHUNK:(cidx + 1) * CHUNK, g * GROUP_DIM:(g + 1) * GROUP_DIM]) + bst_ref[:, g:g + 1]
             for g in range(N_GROUPS)], axis=1))
    s = jnp.concatenate(rows, axis=0)
    oc = _dot((proj("u") * s * _silu(proj("gc"))).astype(BF16), wc_ref[...])

    sma_ref[0] = _sigmoid(proj("ma", d_model)).astype(BF16)
    mbc_ref[0] = (_sigmoid(proj("mb", d_model)) * ob + _sigmoid(proj("mc", d_model)) * oc).astype(BF16)


def _prompt_proj(layer, depth, kv_all, h, w1, w1t, bfc, cw, lvg, lvb, ws, bst, wb, wc, tm, ln_in=None):
    b, s, d = h.shape
    wtot = w1.shape[2]
    const = lambda shape: pl.BlockSpec(shape, lambda bi, i, lr: (0,) * len(shape), pipeline_mode=pl.Buffered(1))
    per_layer = lambda shape: pl.BlockSpec((None,) + shape, lambda bi, i, lr: (lr[0],) + (0,) * len(shape),
                                           pipeline_mode=pl.Buffered(1))
    rows = lambda w: pl.BlockSpec((1, tm, w), lambda bi, i, lr: (bi, i, 0))
    cols = lambda r: pl.BlockSpec((1, r, tm), lambda bi, i, lr: (bi, 0, i))
    heads_r = pl.BlockSpec((1, N_HEADS, tm, LANES), lambda bi, i, lr: (bi, 0, i, 0))
    heads_c = lambda r: pl.BlockSpec((1, N_HEADS, r, tm), lambda bi, i, lr: (bi, 0, 0, i))
    in_specs = [rows(d), per_layer((d, wtot)), per_layer((W1T_ROWS, d)),
                const((BF16_ROWS, LANES)), const((CONV_WIDTH, BRANCH_W)),
                const((1, BRANCH_W)), const((1, BRANCH_W)), const((N_GROUPS, CHUNK, CHUNK)),
                const((CHUNK, N_GROUPS)), const((BRANCH_W, d)), const((BRANCH_W, d))]
    args = [h, w1, w1t, bfc, cw, lvg, lvb, ws, bst, wb, wc]
    extra_out_specs, extra_out_shapes = [], []
    if ln_in is not None:
        in_specs += [const((1, d)), const((1, d))]
        args += [a.reshape(1, d) for a in ln_in]
        extra_out_specs, extra_out_shapes = [rows(d)], [jax.ShapeDtypeStruct((b, s, d), F32)]
    n_in = 1 + len(args)
    in_specs += [pl.BlockSpec(memory_space=pl.ANY)] * 2
    args += list(kv_all)
    aliases = {n_in: 3, n_in + 1: 4}
    grid_spec = pltpu.PrefetchScalarGridSpec(
        num_scalar_prefetch=1,
        grid=(b, s // tm),
        in_specs=in_specs,
        out_specs=[heads_c(LANES), heads_r, heads_c(V_ROWS),
                   pl.BlockSpec((1, 1, tm, BRANCH_W), lambda bi, i, lr: (lr[0], bi, i, 0)),
                   pl.BlockSpec((1, 1, BRANCH_W, tm), lambda bi, i, lr: (lr[0], bi, 0, i)),
                   cols(N_HEADS),
                   pl.BlockSpec((1, CONV_WIDTH - 1, BRANCH_W), lambda bi, i, lr: (bi, 0, 0)),
                   rows(BRANCH_W), rows(d), rows(d)] + extra_out_specs,
        scratch_shapes=[pltpu.VMEM((BF16_ROWS, LANES), F32), pltpu.VMEM((tm + SUBLANES, BRANCH_W), F32)],
    )
    return pl.pallas_call(
        functools.partial(_prompt_proj_kernel, tm=tm, d_model=d, n_in=len(args), ln_in=ln_in is not None),
        grid_spec=grid_spec,
        out_shape=[jax.ShapeDtypeStruct((b, N_HEADS, LANES, s), BF16), jax.ShapeDtypeStruct((b, N_HEADS, s, LANES), BF16),
                   jax.ShapeDtypeStruct((b, N_HEADS, V_ROWS, s), BF16),
                   jax.ShapeDtypeStruct((depth, b, s, BRANCH_W), F32), jax.ShapeDtypeStruct((depth, b, BRANCH_W, s), F32),
                   jax.ShapeDtypeStruct((b, N_HEADS, s), F32),
                   jax.ShapeDtypeStruct((b, CONV_WIDTH - 1, BRANCH_W), F32),
                   jax.ShapeDtypeStruct((b, s, BRANCH_W), BF16), jax.ShapeDtypeStruct((b, s, d), BF16),
                   jax.ShapeDtypeStruct((b, s, d), BF16)] + extra_out_shapes,
        input_output_aliases=aliases,
        compiler_params=pltpu.CompilerParams(dimension_semantics=("arbitrary", "arbitrary"),
                                             vmem_limit_bytes=VMEM_LIMIT),
        name="prompt_proj",
    )(jnp.full((1,), layer, jnp.int32), *args)


def _flash_kernel(qi_ref, ki_ref, qt_ref, k_ref, vt_ref, o_ref, m_ref, acc_ref, *, tq, tk, bsz):
    p = pl.program_id(0)
    qi = qi_ref[p]
    ki = ki_ref[p]
    n_units = bsz * N_HEADS

    @pl.when(ki == 0)
    def _():
        m_ref[...] = jnp.full_like(m_ref, -jnp.inf)
        acc_ref[...] = jnp.zeros_like(acc_ref)

    def body(masked):
        if masked:
            mask = (ki * tk + lax.broadcasted_iota(jnp.int32, (tk, tq), 0)
                    <= qi * tq + lax.broadcasted_iota(jnp.int32, (tk, tq), 1))

        def scores(u):
            b, h = divmod(u, N_HEADS)
            s = _dot(k_ref[b, h], qt_ref[b, h])
            return jnp.where(mask, s, -jnp.inf) if masked else s

        def softmax(u, s):
            m_old = m_ref[u]
            m_new = jnp.maximum(m_old, jnp.max(s, axis=0, keepdims=True))
            m_ref[u] = m_new
            return jnp.exp2(s - m_new).astype(BF16), jnp.exp2(m_old - m_new)

        def accumulate(u, pexp, a):
            b, h = divmod(u, N_HEADS)
            acc_ref[u] = a * acc_ref[u] + _dot(vt_ref[b, h], pexp)

        s_next = scores(0)
        prev = None
        for u in range(n_units):
            s_cur = s_next
            if u + 1 < n_units:
                s_next = scores(u + 1)
            cur = softmax(u, s_cur)
            if prev is not None:
                accumulate(u - 1, *prev)
            prev = cur
        accumulate(n_units - 1, *prev)

    full = (ki + 1) * tk - 1 <= qi * tq

    @pl.when(full)
    def _():
        body(False)

    @pl.when(jnp.logical_not(full))
    def _():
        body(True)

    @pl.when(ki == (qi * tq + tq - 1) // tk)
    def _():
        for b in range(bsz):
            for j in range(N_HEADS // 2):
                a0, a1 = acc_ref[b * N_HEADS + 2 * j], acc_ref[b * N_HEADS + 2 * j + 1]
                yy = jnp.concatenate([a0[:HEAD_DIM] * (1.0 / a0[AUG0:AUG0 + 1]),
                                      a1[:HEAD_DIM] * (1.0 / a1[AUG0:AUG0 + 1])], axis=0)
                o_ref[b, :, j * LANES:(j + 1) * LANES] = yy.T.astype(BF16)


def _flash(qa, ka, va, tq, tk):
    b, _, s, _ = ka.shape
    pairs = [(qi, ki) for qi in range(s // tq) for ki in range((qi * tq + tq - 1) // tk + 1)]
    qi_tab = jnp.asarray(np.array([p[0] for p in pairs], np.int32))
    ki_tab = jnp.asarray(np.array([p[1] for p in pairs], np.int32))
    grid_spec = pltpu.PrefetchScalarGridSpec(
        num_scalar_prefetch=2,
        grid=(len(pairs),),
        in_specs=[pl.BlockSpec((b, N_HEADS, LANES, tq), lambda p, qt, kt: (0, 0, 0, qt[p])),
                  pl.BlockSpec((b, N_HEADS, tk, LANES), lambda p, qt, kt: (0, 0, kt[p], 0)),
                  pl.BlockSpec((b, N_HEADS, V_ROWS, tk), lambda p, qt, kt: (0, 0, 0, kt[p]))],
        out_specs=pl.BlockSpec((b, tq, BRANCH_W), lambda p, qt, kt: (0, qt[p], 0)),
        scratch_shapes=[pltpu.VMEM((b * N_HEADS, 1, tq), F32), pltpu.VMEM((b * N_HEADS, V_ROWS, tq), F32)],
    )
    return pl.pallas_call(
        functools.partial(_flash_kernel, tq=tq, tk=tk, bsz=b),
        grid_spec=grid_spec,
        out_shape=jax.ShapeDtypeStruct((b, s, BRANCH_W), BF16),
        compiler_params=pltpu.CompilerParams(dimension_semantics=("arbitrary",), vmem_limit_bytes=VMEM_LIMIT),
        name="prompt_flash",
    )(qi_tab, ki_tab, qa, ka, va)


def _merge_kernel(h_ref, ya_ref, sga_ref, sma_ref, mbc_ref, wa_ref, wo_ref, g_ref, b_ref, o_ref, *, alpha):
    oa = _dot((ya_ref[...].astype(F32) * sga_ref[...].astype(F32)).astype(BF16), wa_ref[...])
    m = sma_ref[...].astype(F32) * oa + mbc_ref[...].astype(F32)
    x = alpha * h_ref[...] + _dot(m.astype(BF16), wo_ref[...])
    o_ref[...] = _layer_norm(x, g_ref[...], b_ref[...])


def _merge(h, ya, sga, sma, mbc, wa, wo, g, b, alpha, tm):
    n, d = h.shape
    rows = lambda w: pl.BlockSpec((tm, w), lambda i: (i, 0))
    const = lambda shape: pl.BlockSpec(shape, lambda i: (0,) * len(shape))
    return pl.pallas_call(
        functools.partial(_merge_kernel, alpha=alpha),
        grid=(n // tm,),
        in_specs=[rows(d), rows(BRANCH_W), rows(BRANCH_W), rows(d), rows(d),
                  const((BRANCH_W, d)), const((d, d)), const((1, d)), const((1, d))],
        out_specs=rows(d),
        out_shape=jax.ShapeDtypeStruct((n, d), F32),
        compiler_params=pltpu.CompilerParams(dimension_semantics=("arbitrary",), vmem_limit_bytes=VMEM_LIMIT),
        name="merge",
    )(h, ya, sga, sma, mbc, wa, wo, g, b)


def _sample_proj_kernel(h_ref, w1_ref, w1t_ref, bf_ref, cw_ref, lvg_ref, lvb_ref, wst_ref, bsr_ref, wb_ref, wc_ref,
                        zp1_ref, zp2_ref,
                        q_ref, k_ref, v_ref, lf_ref, cnt_ref, z_ref, vn_ref, sga_ref, sma_ref, mbc_ref,
                        p_ref, zbuf_ref, *, tn, n_steps, t_new, d_model):
    off, _ = _offsets(d_model)
    j = pl.program_id(0)
    ns = h_ref.shape[0]
    hb = h_ref[...].astype(BF16)
    col = pl.multiple_of(j * tn, LANES)
    p_ref[:, pl.ds(col, tn)] = _dot(hb, w1_ref[...])

    @pl.when(j == n_steps - 1)
    def _():
        def proj(name, width=BRANCH_W):
            return p_ref[:, off[name]:off[name] + width]

        qv = _dot_nt(hb, w1t_ref[...])
        q_ref[...] = qv[:, 0:BRANCH_W]
        v_ref[...] = qv[:, BRANCH_W:2 * BRANCH_W]
        k_ref[...] = proj("k")
        logf = _log_sigmoid(qv[:, 2 * BRANCH_W:2 * BRANCH_W + FORGET_PAD] + bf_ref[...])
        lf_ref[...] = logf[:, :N_HEADS]
        lt = logf.T
        tpos = lax.broadcasted_iota(jnp.int32, lt.shape, 1) % t_new
        cn = lt
        shift = 1
        while shift < t_new:
            cn = cn + jnp.where(tpos >= shift, pltpu.roll(cn, shift, axis=1), 0.0)
            shift *= 2
        cnt_ref[...] = cn[0:N_HEADS, :]

        sga_ref[...] = _silu(proj("ga")).astype(BF16)

        z = proj("cc") * proj("ch")
        z_ref[...] = z
        zbuf_ref[0:SUBLANES, :] = jnp.zeros((SUBLANES, BRANCH_W), F32)
        zbuf_ref[SUBLANES:SUBLANES + ns, :] = z
        trow = lax.broadcasted_iota(jnp.int32, (ns, BRANCH_W), 0) % t_new
        z1 = jnp.where(trow < 1, zp1_ref[...], zbuf_ref[SUBLANES - 1:SUBLANES - 1 + ns, :])
        z2 = jnp.where(trow < 2, zp2_ref[...], zbuf_ref[SUBLANES - 2:SUBLANES - 2 + ns, :])
        ob = _dot(_conv_gate(z, z1, z2, proj("cb"), proj("gb"), cw_ref), wb_ref[...])

        vn = _layer_norm(proj("vc"), lvg_ref[...], lvb_ref[...])
        vn_ref[...] = vn
        vnb = vn.astype(BF16)
        r = lax.broadcasted_iota(jnp.int32, (ns, ns), 0)
        c = lax.broadcasted_iota(jnp.int32, (ns, ns), 1)
        keep = (r // t_new == c // t_new) & (c % t_new <= r % t_new)
        s = jnp.concatenate(
            [_dot(jnp.where(keep, wst_ref[g], 0.0).astype(BF16), vnb[:, g * GROUP_DIM:(g + 1) * GROUP_DIM])
             + bsr_ref[:, g:g + 1] for g in range(N_GROUPS)], axis=1)
        oc = _dot((proj("u") * s * _silu(proj("gc"))).astype(BF16), wc_ref[...])

        sma_ref[...] = _sigmoid(proj("ma", d_model)).astype(BF16)
        mbc_ref[...] = (_sigmoid(proj("mb", d_model)) * ob + _sigmoid(proj("mc", d_model)) * oc).astype(BF16)


def _sample_proj(layer, h, w1, w1t, bf, cw, lvg, lvb, wst, bsr, wb, wc, zp1, zp2, t_new):
    ns, d = h.shape
    wtot = w1.shape[2]
    tn = W1_ALIGN
    n_steps = wtot // tn
    const = lambda shape: pl.BlockSpec(shape, lambda j: (0,) * len(shape))
    f32 = lambda w: jax.ShapeDtypeStruct((ns, w), F32)
    bf16 = lambda w: jax.ShapeDtypeStruct((ns, w), BF16)
    return pl.pallas_call(
        functools.partial(_sample_proj_kernel, tn=tn, n_steps=n_steps, t_new=t_new, d_model=d),
        grid=(n_steps,),
        in_specs=[const((ns, d)), pl.BlockSpec((None, d, tn), lambda j: (layer, 0, j)),
                  pl.BlockSpec((None, W1T_ROWS, d), lambda j: (layer, 0, 0)), const((1, FORGET_PAD)),
                  const((CONV_WIDTH, BRANCH_W)), const((1, BRANCH_W)), const((1, BRANCH_W)),
                  const((N_GROUPS, ns, ns)), const((ns, N_GROUPS)), const((BRANCH_W, d)), const((BRANCH_W, d)),
                  const((ns, BRANCH_W)), const((ns, BRANCH_W))],
        out_specs=[const((ns, BRANCH_W)), const((ns, BRANCH_W)), const((ns, BRANCH_W)), const((ns, N_HEADS)),
                   const((N_HEADS, ns)), const((ns, BRANCH_W)), const((ns, BRANCH_W)),
                   const((ns, BRANCH_W)), const((ns, d)), const((ns, d))],
        out_shape=[f32(BRANCH_W), f32(BRANCH_W), f32(BRANCH_W), f32(N_HEADS),
                   jax.ShapeDtypeStruct((N_HEADS, ns), F32), f32(BRANCH_W), f32(BRANCH_W),
                   bf16(BRANCH_W), bf16(d), bf16(d)],
        scratch_shapes=[pltpu.VMEM((ns, wtot), F32), pltpu.VMEM((ns + SUBLANES, BRANCH_W), F32)],
        compiler_params=pltpu.CompilerParams(dimension_semantics=("arbitrary",), vmem_limit_bytes=VMEM_LIMIT),
        name="sample_proj",
    )(h, w1, w1t, bf, cw, lvg, lvb, wst, bsr, wb, wc, zp1, zp2)


def _sample_attn_kernel(rows_ref, q_ref, kn_ref, vn_ref, cn_ref, ck_hbm, cv_hbm, clf_hbm, o_ref,
                        kbuf, vbuf, lbuf, sem, m_ref, l_ref, acc_ref, car_ref, *, pp, t_new):
    step = pl.program_id(1)
    n_steps = pl.num_programs(1)
    g = pl.program_id(0) * n_steps + step
    total = pl.num_programs(0) * n_steps
    slot = g % 2
    tpad = q_ref.shape[2]

    def page_copies(gg, sl):
        out = []
        for i in range(pp):
            row = rows_ref[gg * pp + i]
            out.append(pltpu.make_async_copy(ck_hbm.at[row], kbuf.at[sl, i], sem.at[sl, 0]))
            out.append(pltpu.make_async_copy(cv_hbm.at[row], vbuf.at[sl, i], sem.at[sl, 1]))
            out.append(pltpu.make_async_copy(clf_hbm.at[row], lbuf.at[sl, i], sem.at[sl, 2]))
        return out

    @pl.when(g == 0)
    def _():
        for c in page_copies(0, 0):
            c.start()

    @pl.when(g + 1 < total)
    def _():
        for c in page_copies(g + 1, 1 - slot):
            c.start()

    @pl.when(step == 0)
    def _():
        car_ref[...] = jnp.zeros_like(car_ref)
        q3 = q_ref[0] * SCALE
        trow = lax.broadcasted_iota(jnp.int32, (N_HEADS, tpad, 1), 1)
        ss = []
        for t in range(t_new):
            st = jnp.sum(q3 * kn_ref[0, :, t:t + 1, :], axis=2, keepdims=True) - cn_ref[0, :, t:t + 1][:, :, None]
            ss.append(jnp.where(trow >= t, st, -jnp.inf))
        m = ss[0]
        for st in ss[1:]:
            m = jnp.maximum(m, st)
        l = jnp.zeros((N_HEADS, tpad, 1), F32)
        acc = jnp.zeros((N_HEADS, tpad, HEAD_DIM), F32)
        for t in range(t_new):
            pt = jnp.exp(ss[t] - m)
            l = l + pt
            acc = acc + pt * vn_ref[0, :, t:t + 1, :]
        m_ref[...] = m
        l_ref[...] = l
        acc_ref[...] = acc

    for c in page_copies(g, slot):
        c.wait()

    ci = lax.broadcasted_iota(jnp.int32, (LANES, 2 * LANES), 0)
    cj = lax.broadcasted_iota(jnp.int32, (LANES, 2 * LANES), 1)
    usum = jnp.where((cj >= LANES) | (ci > cj), 1.0, 0.0).astype(BF16)
    carry = car_ref[...]
    bias = []
    for i in range(pp):
        r = jnp.zeros((N_HEADS, 2 * LANES), F32)
        for piece in _split3(lbuf[slot, i]):
            r = r + _dot(piece, usum)
        bias.append(r[:, :LANES] + carry)
        carry = carry + r[:, LANES:]
    car_ref[...] = carry
    bias = jnp.concatenate(bias, axis=1)

    ss = []
    for h in range(N_HEADS):
        qh = (q_ref[0, h] * SCALE).astype(BF16)
        kt = jnp.concatenate([kbuf[slot, i, h * HEAD_DIM:(h + 1) * HEAD_DIM, :] for i in range(pp)],
                             axis=1).astype(BF16)
        ss.append(_dot(qh, kt) + bias[h:h + 1, :])
    ps, al = [], []
    for h in range(N_HEADS):
        m_old = m_ref[h]
        m_new = jnp.maximum(m_old, jnp.max(ss[h], axis=1, keepdims=True))
        a = jnp.exp(m_old - m_new)
        p = jnp.exp(ss[h] - m_new)
        l_ref[h] = a * l_ref[h] + jnp.sum(p, axis=1, keepdims=True)
        m_ref[h] = m_new
        ps.append(p.astype(BF16))
        al.append(a)
    for h in range(N_HEADS):
        vt = jnp.concatenate([vbuf[slot, i, h * HEAD_DIM:(h + 1) * HEAD_DIM, :] for i in range(pp)],
                             axis=1).astype(BF16)
        acc_ref[h] = al[h] * acc_ref[h] + _dot_nt(ps[h], vt)

    @pl.when(step == n_steps - 1)
    def _():
        for h in range(N_HEADS):
            o_ref[0, h] = acc_ref[h] * (1.0 / l_ref[h])


def _sample_attn(rows, q, kn, vn, cn, cache_kt, cache_vt, cache_lft, n_pages, t_new):
    nb, tpad = q.shape[0], q.shape[2]
    width, page = cache_kt.shape[1], cache_kt.shape[2]
    pp = PAGES_PER_STEP if n_pages % PAGES_PER_STEP == 0 else n_pages
    n_steps = n_pages // pp
    per_seq = lambda shape: pl.BlockSpec((1,) + shape, lambda b, s, rref: (b,) + (0,) * len(shape))
    in_hbm = pl.BlockSpec(memory_space=pl.ANY)
    grid_spec = pltpu.PrefetchScalarGridSpec(
        num_scalar_prefetch=1,
        grid=(nb, n_steps),
        in_specs=[per_seq((N_HEADS, tpad, HEAD_DIM))] * 3 + [per_seq((N_HEADS, LANES))] + [in_hbm] * 3,
        out_specs=per_seq((N_HEADS, tpad, HEAD_DIM)),
        scratch_shapes=[pltpu.VMEM((2, pp, width, page), F32), pltpu.VMEM((2, pp, width, page), F32),
                        pltpu.VMEM((2, pp, N_HEADS, page), F32), pltpu.SemaphoreType.DMA((2, 3)),
                        pltpu.VMEM((N_HEADS, tpad, 1), F32), pltpu.VMEM((N_HEADS, tpad, 1), F32),
                        pltpu.VMEM((N_HEADS, tpad, HEAD_DIM), F32), pltpu.VMEM((N_HEADS, LANES), F32)],
    )
    return pl.pallas_call(
        functools.partial(_sample_attn_kernel, pp=pp, t_new=t_new),
        grid_spec=grid_spec,
        out_shape=jax.ShapeDtypeStruct((nb, N_HEADS, tpad, HEAD_DIM), F32),
        compiler_params=pltpu.CompilerParams(dimension_semantics=("arbitrary", "arbitrary"),
                                             vmem_limit_bytes=VMEM_LIMIT),
        name="sample_attn",
    )(rows, q, kn, vn, cn, cache_kt, cache_vt, cache_lft)


def _pick_tile(n, pref):
    t = pref
    while n % t:
        t //= 2
    return t


def kernel(x_prompt, x_sample, cache_k, cache_v, cache_logf, state_conv, page_table, ln_in_g, ln_in_b, w_in, b_f,
           conv_w, ln_v_g, ln_v_b, w_s, b_s, w_a_out, w_b_out, w_c_out, w_o, ln_g, ln_b):
    depth, d = w_in.shape[0], w_in.shape[1]
    bsz, seq, _ = x_prompt.shape
    nb, t_new, _ = x_sample.shape
    ns = nb * t_new
    assert seq % CHUNK == 0 and t_new <= SUBLANES and page_table.shape[0] == nb
    alpha = (2.0 * depth) ** 0.25
    tm = _pick_tile(seq, 512)
    tq = _pick_tile(seq, 512)

    _, wtot = _offsets(d)
    q0, k0, v0, f0, g0 = 0, BRANCH_W, 2 * BRANCH_W, 3 * BRANCH_W, 3 * BRANCH_W + N_HEADS
    w_int = jnp.swapaxes(w_in, 1, 2)
    w1, w1t = _prep_weights(w_int, wtot)
    b_f_pad = jnp.pad(b_f, ((0, 0), (0, FORGET_PAD - N_HEADS)))
    bfr = b_f_pad.reshape(depth, 1, FORGET_PAD)
    bfc = jnp.broadcast_to(b_f_pad[:, :BF16_ROWS, None], (depth, BF16_ROWS, LANES))
    wa, wb, wc, wo = (w.astype(BF16) for w in (w_a_out, w_b_out, w_c_out, w_o))
    bst = jnp.swapaxes(b_s, 1, 2)
    wst = jnp.tile(w_s[:, :, :t_new, :t_new], (1, 1, ns // t_new, ns // t_new))
    bsr = jnp.tile(jnp.swapaxes(b_s[:, :, :t_new], 1, 2), (1, ns // t_new, 1))
    pool, page = cache_k.shape[1], cache_k.shape[2]
    n_pages = page_table.shape[1]
    cache_kt = jnp.transpose(cache_k, (0, 1, 3, 4, 2)).reshape(depth * pool, BRANCH_W, page)
    cache_vt = jnp.transpose(cache_v, (0, 1, 3, 4, 2)).reshape(depth * pool, BRANCH_W, page)
    cache_lft = jnp.swapaxes(cache_logf, 2, 3).reshape(depth * pool, N_HEADS, page)
    pages_latest_first = page_table[:, ::-1].reshape(-1)
    zp1 = jnp.concatenate([state_conv[:, :, 1:2], jnp.zeros((depth, nb, t_new - 1, BRANCH_W), F32)], axis=2)
    zp2 = jnp.concatenate([state_conv, jnp.zeros((depth, nb, t_new - 2, BRANCH_W), F32)], axis=2)
    zp1, zp2 = zp1.reshape(depth, ns, BRANCH_W), zp2.reshape(depth, ns, BRANCH_W)

    hp = x_prompt
    hs = _ln_rows(x_sample.reshape(ns, d), ln_in_g, ln_in_b, ns)

    outs = {n: [] for n in ("lp", "cp", "ks", "vs", "ls", "cs", "us")}
    kv_all = (jnp.zeros((depth, bsz, seq, BRANCH_W), F32), jnp.zeros((depth, bsz, BRANCH_W, seq), F32))
    for l in range(depth):
        row1 = lambda a: a[l].reshape(1, -1)
        qa, ka, va, ko_all, vo_all, lf, tail, sga, sma, mbc, *hn = _prompt_proj(
            l, depth, kv_all, hp, w1, w1t, bfc[l], conv_w[l], row1(ln_v_g), row1(ln_v_b), w_s[l], bst[l],
            wb[l], wc[l], tm, ln_in=(ln_in_g, ln_in_b) if l == 0 else None)
        if hn:
            hp = hn[0]
        kv_all = (ko_all, vo_all)
        ya = _flash(qa, ka, va, tq, tq)
        flat = lambda a: a.reshape(bsz * seq, a.shape[-1])
        hp = _merge(flat(hp), flat(ya), flat(sga), flat(sma), flat(mbc), wa[l], wo[l], row1(ln_g), row1(ln_b),
                    alpha, tm).reshape(bsz, seq, d)
        outs["lp"].append(lf); outs["cp"].append(tail)
        qs, ks, vs, lfs, cnt, zs, vns, sga, sma, mbc = _sample_proj(
            l, hs, w1, w1t, bfr[l], conv_w[l], row1(ln_v_g), row1(ln_v_b), wst[l], bsr[l], wb[l], wc[l],
            zp1[l], zp2[l], t_new)
        head_major = lambda a: jnp.pad(jnp.transpose(a.reshape(nb, t_new, N_HEADS, HEAD_DIM), (0, 2, 1, 3)),
                                       ((0, 0), (0, 0), (0, SUBLANES - t_new), (0, 0)))
        cn = jnp.pad(jnp.transpose(cnt.reshape(N_HEADS, nb, t_new), (1, 0, 2)), ((0, 0), (0, 0), (0, LANES - t_new)))
        yas = _sample_attn(pages_latest_first + l * pool, head_major(qs), head_major(ks), head_major(vs),
                           cn, cache_kt, cache_vt, cache_lft, n_pages, t_new)
        yas = jnp.transpose(yas[:, :, :t_new], (0, 2, 1, 3)).reshape(ns, BRANCH_W)
        hs = _merge(hs, yas.astype(BF16), sga, sma, mbc, wa[l], wo[l], row1(ln_g), row1(ln_b), alpha, ns)
        outs["ks"].append(ks); outs["vs"].append(vs); outs["ls"].append(lfs)
        outs["cs"].append(zs.reshape(nb, t_new, BRANCH_W)[:, t_new - (CONV_WIDTH - 1):])
        outs["us"].append(vns)

    st = {n: jnp.stack(v) for n, v in outs.items()}
    ko_all, vo_all = kv_all
    new_v_prompt = jnp.transpose(vo_all.reshape(depth, bsz, N_HEADS, HEAD_DIM, seq), (0, 1, 4, 2, 3))
    return (hp, hs.reshape(nb, t_new, d),
            ko_all.reshape(depth, bsz, seq, N_HEADS, HEAD_DIM), new_v_prompt,
            jnp.swapaxes(st["lp"], 2, 3), st["cp"],
            st["ks"].reshape(depth, nb, t_new, N_HEADS, HEAD_DIM), st["vs"].reshape(depth, nb, t_new, N_HEADS, HEAD_DIM),
            st["ls"].reshape(depth, nb, t_new, N_HEADS), st["cs"], st["us"].reshape(depth, nb, t_new, BRANCH_W))
```

```python
import functools
import math

import numpy as np
import jax
import jax.numpy as jnp
from jax import lax
from jax.experimental import pallas as pl
from jax.experimental.pallas import tpu as pltpu

F32 = jnp.float32
BF16 = jnp.bfloat16

N_HEADS = 8
HEAD_DIM = 64
BRANCH_W = 512
CHUNK = 128
N_GROUPS = 4
GROUP_DIM = BRANCH_W // N_GROUPS
CONV_WIDTH = 3
LN_EPS = 1e-5
LANES = 128
SUBLANES = 8
BF16_ROWS = 16
FORGET_PAD = LANES
ROW_SLOTS = ("k", "ga", "cb", "cc", "ch", "gb", "u", "vc", "gc")
W1_ALIGN = 2048
W1_CHUNK = 256
W1T_ROWS = 2 * BRANCH_W + W1_CHUNK
SCALE = HEAD_DIM ** -0.5
LOG2E = math.log2(math.e)
VMEM_LIMIT = 56 * 1024 * 1024
PAGES_PER_STEP = 16

AUG0 = HEAD_DIM
V_ROWS = LANES


def _offsets(d_model):
    off = {n: i * BRANCH_W for i, n in enumerate(ROW_SLOTS)}
    base = len(ROW_SLOTS) * BRANCH_W
    off["ma"], off["mb"], off["mc"] = base, base + d_model, base + 2 * d_model
    used = base + 3 * d_model
    return off, -(-used // W1_ALIGN) * W1_ALIGN


def _dot(a, b):
    return jnp.dot(a, b, preferred_element_type=F32)


def _dot_nt(a, b):
    return lax.dot_general(a, b, (((1,), (1,)), ((), ())), preferred_element_type=F32)


def _split3(x):
    hi = x.astype(BF16)
    r = x - hi.astype(F32)
    mid = r.astype(BF16)
    lo = (r - mid.astype(F32)).astype(BF16)
    return hi, mid, lo


def _sigmoid(x):
    return 1.0 / (1.0 + jnp.exp(-x))


def _silu(x):
    return x * _sigmoid(x)


def _log_sigmoid(x):
    return jnp.minimum(x, 0.0) - jnp.log1p(jnp.exp(-jnp.abs(x)))


def _layer_norm(x, g, b):
    mu = jnp.mean(x, axis=-1, keepdims=True)
    xc = x - mu
    var = jnp.mean(xc * xc, axis=-1, keepdims=True)
    return xc * lax.rsqrt(var + LN_EPS) * g + b


def _head_tiles(x):
    out = []
    for j in range(BRANCH_W // LANES):
        blk = x[:, j * LANES:(j + 1) * LANES]
        out.append(blk)
        out.append(pltpu.roll(blk, HEAD_DIM, axis=1))
    return out


def _ln_kernel(x_ref, g_ref, b_ref, o_ref):
    o_ref[...] = _layer_norm(x_ref[...], g_ref[...], b_ref[...])


def _ln_rows(x, g, b, tm):
    n, d = x.shape
    return pl.pallas_call(
        _ln_kernel,
        grid=(n // tm,),
        in_specs=[pl.BlockSpec((tm, d), lambda i: (i, 0)),
                  pl.BlockSpec((1, d), lambda i: (0, 0)),
                  pl.BlockSpec((1, d), lambda i: (0, 0))],
        out_specs=pl.BlockSpec((tm, d), lambda i: (i, 0)),
        out_shape=jax.ShapeDtypeStruct((n, d), F32),
        name="ln_in",
    )(x, g.reshape(1, d), b.reshape(1, d))


def _w1_kernel(a_ref, b_ref, o_ref, *, n_k, n_real, shift):
    j = pl.program_id(1)

    @pl.when(j < n_k)
    def _():
        o_ref[0] = a_ref[0].T.astype(BF16)

    @pl.when((j >= n_k) & (j < n_real))
    def _():
        o_ref[0] = jnp.concatenate([a_ref[0, shift:, :], b_ref[0]], axis=0).T.astype(BF16)

    @pl.when(j >= n_real)
    def _():
        o_ref[0] = jnp.zeros(o_ref.shape[1:], BF16)


def _w1t_kernel(a_ref, o_ref, *, n_chunks):
    c = pl.program_id(1)
    x = a_ref[0]
    keep = (c < n_chunks - 1) | (lax.broadcasted_iota(jnp.int32, x.shape, 0) < N_HEADS)
    o_ref[0] = jnp.where(keep, x, 0.0).astype(BF16)


def _prep_weights(w_int, wtot):
    depth, p_total, d = w_int.shape
    k0, v0, f0 = BRANCH_W, 2 * BRANCH_W, 3 * BRANCH_W
    g0 = f0 + N_HEADS
    n_k = BRANCH_W // W1_CHUNK
    n_real = n_k + (p_total - g0) // W1_CHUNK
    shift = g0 % W1_CHUNK
    assert (p_total - g0) % W1_CHUNK == 0 and wtot % W1_CHUNK == 0 and 0 < shift and W1_CHUNK % shift == 0
    assert shift % SUBLANES == 0 and p_total % shift == 0
    per_chunk = W1_CHUNK // shift

    def blk_a(j):
        jj = jnp.minimum(j, n_real - 1)
        return jnp.where(jj < n_k, jj + k0 // W1_CHUNK, jj - n_k + g0 // W1_CHUNK)

    w1 = pl.pallas_call(
        functools.partial(_w1_kernel, n_k=n_k, n_real=n_real, shift=shift),
        grid=(depth, wtot // W1_CHUNK),
        in_specs=[pl.BlockSpec((1, W1_CHUNK, d), lambda l, j: (l, blk_a(j), 0)),
                  pl.BlockSpec((1, shift, d),
                               lambda l, j: (l, jnp.minimum((blk_a(j) + 1) * per_chunk, p_total // shift - 1), 0))],
        out_specs=pl.BlockSpec((1, d, W1_CHUNK), lambda l, j: (l, 0, j)),
        out_shape=jax.ShapeDtypeStruct((depth, d, wtot), BF16),
        name="w1_rowmajor",
    )(w_int, w_int)
    n_chunks = W1T_ROWS // W1_CHUNK
    n_q = BRANCH_W // W1_CHUNK
    w1t = pl.pallas_call(
        functools.partial(_w1t_kernel, n_chunks=n_chunks),
        grid=(depth, n_chunks),
        in_specs=[pl.BlockSpec((1, W1_CHUNK, d),
                               lambda l, c: (l, jnp.where(c < n_q, c, c - n_q + v0 // W1_CHUNK), 0))],
        out_specs=pl.BlockSpec((1, W1_CHUNK, d), lambda l, c: (l, c, 0)),
        out_shape=jax.ShapeDtypeStruct((depth, W1T_ROWS, d), BF16),
        name="w1_transposed",
    )(w_int)
    return w1, w1t


def _conv_gate(z, z1, z2, cb, gb, cw_ref):
    y = cw_ref[0:1, :] * z2 + cw_ref[1:2, :] * z1 + cw_ref[2:3, :] * z
    return (cb * y * _silu(gb)).astype(BF16)


def _prompt_proj_kernel(layer_ref, *refs, tm, d_model, n_in, ln_in):
    del layer_ref
    (h_ref, w1_ref, w1t_ref, bfc_ref, cw_ref, lvg_ref, lvb_ref, ws_ref, bst_ref, wb_ref, wc_ref) = refs[:11]
    (qa_ref, ka_ref, va_ref, ko_ref, vo_ref, lf_ref, tail_ref, sga_ref, sma_ref, mbc_ref) = refs[n_in:n_in + 10]
    ccol_ref, zbuf_ref = refs[-2:]
    off, _ = _offsets(d_model)
    i = pl.program_id(1)

    @pl.when(i == 0)
    def _():
        ccol_ref[...] = jnp.zeros_like(ccol_ref)
        zbuf_ref[0:SUBLANES, :] = jnp.zeros((SUBLANES, BRANCH_W), F32)

    if ln_in:
        lng_ref, lnb_ref, hn_ref = refs[11], refs[12], refs[n_in + 10]
        hn = _layer_norm(h_ref[0], lng_ref[...], lnb_ref[...])
        hn_ref[0] = hn
        hb = hn.astype(BF16)
    else:
        hb = h_ref[0].astype(BF16)

    def proj(name, width=BRANCH_W):
        return _dot(hb, w1_ref[:, off[name]:off[name] + width])

    pt = _dot_nt(w1t_ref[0:2 * BRANCH_W + BF16_ROWS, :], hb)

    logf_c = _log_sigmoid(pt[2 * BRANCH_W:2 * BRANCH_W + BF16_ROWS] + bfc_ref[:, 0:1])
    lf_ref[0] = logf_c[0:N_HEADS]
    upper = jnp.where(lax.broadcasted_iota(jnp.int32, (tm, tm), 0) <= lax.broadcasted_iota(jnp.int32, (tm, tm), 1),
                      1.0, 0.0).astype(BF16)
    c_c = ccol_ref[:, 0:1]
    for piece in _split3(logf_c):
        c_c = c_c + _dot(piece, upper)
    ccol_ref[...] = jnp.broadcast_to(c_c[:, tm - 1:tm], ccol_ref.shape)
    c_c = c_c * LOG2E
    c3c = [p.astype(F32) for p in _split3(c_c)]
    c_r = jnp.concatenate([c_c, jnp.zeros((LANES - BF16_ROWS, tm), F32)], axis=0).T
    c3r = [p.astype(F32) for p in _split3(c_r)]

    k = proj("k")
    ko_ref[0, 0] = k
    vo_ref[0, 0] = pt[BRANCH_W:2 * BRANCH_W]
    kh = _head_tiles(k)
    lane = lax.broadcasted_iota(jnp.int32, (tm, LANES), 1)
    row8 = lax.broadcasted_iota(jnp.int32, (SUBLANES, tm), 0)
    zpad = jnp.zeros((LANES - HEAD_DIM - SUBLANES, tm), F32)
    vpad = jnp.zeros((V_ROWS - HEAD_DIM - SUBLANES, tm), F32)
    ones_row = jnp.where(row8 == 0, 1.0, 0.0)
    for h in range(N_HEADS):
        cb3 = [jnp.broadcast_to(p[:, h:h + 1], (tm, LANES)) for p in c3r]
        ak = jnp.where(lane < AUG0 + 3, 1.0, jnp.where(lane == AUG0 + 3, -cb3[0], jnp.where(
            lane == AUG0 + 4, -cb3[1], jnp.where(lane == AUG0 + 5, -cb3[2], 0.0))))
        ka_ref[0, h] = jnp.where(lane < HEAD_DIM, kh[h], ak).astype(BF16)
        aq = jnp.where(row8 == 0, c3c[0][h:h + 1], jnp.where(row8 == 1, c3c[1][h:h + 1], jnp.where(
            row8 == 2, c3c[2][h:h + 1], jnp.where(row8 < 6, 1.0, 0.0))))
        qt = pt[h * HEAD_DIM:(h + 1) * HEAD_DIM] * (SCALE * LOG2E)
        qa_ref[0, h] = jnp.concatenate([qt, aq, zpad], axis=0).astype(BF16)
        vt = pt[BRANCH_W + h * HEAD_DIM:BRANCH_W + (h + 1) * HEAD_DIM]
        va_ref[0, h] = jnp.concatenate([vt, ones_row, vpad], axis=0).astype(BF16)

    sga_ref[0] = _silu(proj("ga")).astype(BF16)

    z = proj("cc") * proj("ch")
    zbuf_ref[SUBLANES:SUBLANES + tm, :] = z
    z1 = zbuf_ref[SUBLANES - 1:SUBLANES - 1 + tm, :]
    z2 = zbuf_ref[SUBLANES - 2:SUBLANES - 2 + tm, :]
    zbuf_ref[0:SUBLANES, :] = z[tm - SUBLANES:tm, :]
    tail_ref[0] = z[tm - (CONV_WIDTH - 1):tm, :]
    ob = _dot(_conv_gate(z, z1, z2, proj("cb"), proj("gb"), cw_ref), wb_ref[...])

    vn = _layer_norm(proj("vc"), lvg_ref[...], lvb_ref[...]).astype(BF16)
    tril = lax.broadcasted_iota(jnp.int32, (CHUNK, CHUNK), 0) >= lax.broadcasted_iota(jnp.int32, (CHUNK, CHUNK), 1)
    wm = [jnp.where(tril, ws_ref[g], 0.0).astype(BF16) for g in range(N_GROUPS)]
    rows = []
    for cidx in range(tm // CHUNK):
        rows.append(jnp.concatenate(
            [_dot(wm[g], vn[cidx * CHUNK:(cidx + 1) * CHUNK, g * GROUP_DIM:(g + 1) * GROUP_DIM]) + bst_ref[:, g:g + 1]
             for g in range(N_GROUPS)], axis=1))
    s = jnp.concatenate(rows, axis=0)
    oc = _dot((proj("u") * s * _silu(proj("gc"))).astype(BF16), wc_ref[...])

    sma_ref[0] = _sigmoid(proj("ma", d_model)).astype(BF16)
    mbc_ref[0] = (_sigmoid(proj("mb", d_model)) * ob + _sigmoid(proj("mc", d_model)) * oc).astype(BF16)


def _prompt_proj(layer, depth, kv_all, h, w1, w1t, bfc, cw, lvg, lvb, ws, bst, wb, wc, tm, ln_in=None):
    b, s, d = h.shape
    wtot = w1.shape[2]
    const = lambda shape: pl.BlockSpec(shape, lambda bi, i, lr: (0,) * len(shape), pipeline_mode=pl.Buffered(1))
    per_layer = lambda shape: pl.BlockSpec((None,) + shape, lambda bi, i, lr: (lr[0],) + (0,) * len(shape),
                                           pipeline_mode=pl.Buffered(1))
    rows = lambda w: pl.BlockSpec((1, tm, w), lambda bi, i, lr: (bi, i, 0))
    cols = lambda r: pl.BlockSpec((1, r, tm), lambda bi, i, lr: (bi, 0, i))
    heads_r = pl.BlockSpec((1, N_HEADS, tm, LANES), lambda bi, i, lr: (bi, 0, i, 0))
    heads_c = lambda r: pl.BlockSpec((1, N_HEADS, r, tm), lambda bi, i, lr: (bi, 0, 0, i))
    in_specs = [rows(d), per_layer((d, wtot)), per_layer((W1T_ROWS, d)),
                const((BF16_ROWS, LANES)), const((CONV_WIDTH, BRANCH_W)),
                const((1, BRANCH_W)), const((1, BRANCH_W)), const((N_GROUPS, CHUNK, CHUNK)),
                const((CHUNK, N_GROUPS)), const((BRANCH_W, d)), const((BRANCH_W, d))]
    args = [h, w1, w1t, bfc, cw, lvg, lvb, ws, bst, wb, wc]
    extra_out_specs, extra_out_shapes = [], []
    if ln_in is not None:
        in_specs += [const((1, d)), const((1, d))]
        args += [a.reshape(1, d) for a in ln_in]
        extra_out_specs, extra_out_shapes = [rows(d)], [jax.ShapeDtypeStruct((b, s, d), F32)]
    n_in = 1 + len(args)
    in_specs += [pl.BlockSpec(memory_space=pl.ANY)] * 2
    args += list(kv_all)
    aliases = {n_in: 3, n_in + 1: 4}
    grid_spec = pltpu.PrefetchScalarGridSpec(
        num_scalar_prefetch=1,
        grid=(b, s // tm),
        in_specs=in_specs,
        out_specs=[heads_c(LANES), heads_r, heads_c(V_ROWS),
                   pl.BlockSpec((1, 1, tm, BRANCH_W), lambda bi, i, lr: (lr[0], bi, i, 0)),
                   pl.BlockSpec((1, 1, BRANCH_W, tm), lambda bi, i, lr: (lr[0], bi, 0, i)),
                   cols(N_HEADS),
                   pl.BlockSpec((1, CONV_WIDTH - 1, BRANCH_W), lambda bi, i, lr: (bi, 0, 0)),
                   rows(BRANCH_W), rows(d), rows(d)] + extra_out_specs,
        scratch_shapes=[pltpu.VMEM((BF16_ROWS, LANES), F32), pltpu.VMEM((tm + SUBLANES, BRANCH_W), F32)],
    )
    return pl.pallas_call(
        functools.partial(_prompt_proj_kernel, tm=tm, d_model=d, n_in=len(args), ln_in=ln_in is not None),
        grid_spec=grid_spec,
        out_shape=[jax.ShapeDtypeStruct((b, N_HEADS, LANES, s), BF16), jax.ShapeDtypeStruct((b, N_HEADS, s, LANES), BF16),
                   jax.ShapeDtypeStruct((b, N_HEADS, V_ROWS, s), BF16),
                   jax.ShapeDtypeStruct((depth, b, s, BRANCH_W), F32), jax.ShapeDtypeStruct((depth, b, BRANCH_W, s), F32),
                   jax.ShapeDtypeStruct((b, N_HEADS, s), F32),
                   jax.ShapeDtypeStruct((b, CONV_WIDTH - 1, BRANCH_W), F32),
                   jax.ShapeDtypeStruct((b, s, BRANCH_W), BF16), jax.ShapeDtypeStruct((b, s, d), BF16),
                   jax.ShapeDtypeStruct((b, s, d), BF16)] + extra_out_shapes,
        input_output_aliases=aliases,
        compiler_params=pltpu.CompilerParams(dimension_semantics=("arbitrary", "arbitrary"),
                                             vmem_limit_bytes=VMEM_LIMIT),
        name="prompt_proj",
    )(jnp.full((1,), layer, jnp.int32), *args)


def _flash_kernel(qi_ref, ki_ref, qt_ref, k_ref, vt_ref, o_ref, m_ref, acc_ref, *, tq, tk, bsz):
    p = pl.program_id(0)
    qi = qi_ref[p]
    ki = ki_ref[p]
    n_units = bsz * N_HEADS

    @pl.when(ki == 0)
    def _():
        m_ref[...] = jnp.full_like(m_ref, -jnp.inf)
        acc_ref[...] = jnp.zeros_like(acc_ref)

    def body(masked):
        if masked:
            mask = (ki * tk + lax.broadcasted_iota(jnp.int32, (tk, tq), 0)
                    <= qi * tq + lax.broadcasted_iota(jnp.int32, (tk, tq), 1))

        def scores(u):
            b, h = divmod(u, N_HEADS)
            s = _dot(k_ref[b, h], qt_ref[b, h])
            return jnp.where(mask, s, -jnp.inf) if masked else s

        def softmax(u, s):
            m_old = m_ref[u]
            m_new = jnp.maximum(m_old, jnp.max(s, axis=0, keepdims=True))
            m_ref[u] = m_new
            return jnp.exp2(s - m_new).astype(BF16), jnp.exp2(m_old - m_new)

        def accumulate(u, pexp, a):
            b, h = divmod(u, N_HEADS)
            acc_ref[u] = a * acc_ref[u] + _dot(vt_ref[b, h], pexp)

        s_next = scores(0)
        prev = None
        for u in range(n_units):
            s_cur = s_next
            if u + 1 < n_units:
                s_next = scores(u + 1)
            cur = softmax(u, s_cur)
            if prev is not None:
                accumulate(u - 1, *prev)
            prev = cur
        accumulate(n_units - 1, *prev)

    def diagonal_body():
        hk, hq = tk // 2, tq // 2
        mask1 = lax.broadcasted_iota(jnp.int32, (hk, tq), 0) <= lax.broadcasted_iota(jnp.int32, (hk, tq), 1)
        mask2 = lax.broadcasted_iota(jnp.int32, (hk, hq), 0) <= lax.broadcasted_iota(jnp.int32, (hk, hq), 1)

        def scores(u):
            b, h = divmod(u, N_HEADS)
            s1 = jnp.where(mask1, _dot(k_ref[b, h, 0:hk, :], qt_ref[b, h]), -jnp.inf)
            s2 = jnp.where(mask2, _dot(k_ref[b, h, hk:tk, :], qt_ref[b, h, :, hq:tq]), -jnp.inf)
            return s1, s2

        def softmax(u, s12):
            s1, s2 = s12
            m_old = m_ref[u]
            m1 = jnp.max(s1, axis=0, keepdims=True)
            m2 = jnp.max(s2, axis=0, keepdims=True)
            m_new = jnp.maximum(m_old, jnp.concatenate([m1[:, :hq], jnp.maximum(m1[:, hq:], m2)], axis=1))
            m_ref[u] = m_new
            return ((jnp.exp2(s1 - m_new).astype(BF16), jnp.exp2(s2 - m_new[:, hq:]).astype(BF16)),
                    jnp.exp2(m_old - m_new))

        def accumulate(u, p12, a):
            b, h = divmod(u, N_HEADS)
            p1, p2 = p12
            upd1 = _dot(vt_ref[b, h, :, 0:hk], p1)
            upd2 = _dot(vt_ref[b, h, :, hk:tk], p2)
            acc_ref[u] = a * acc_ref[u] + jnp.concatenate([upd1[:, :hq], upd1[:, hq:] + upd2], axis=1)

        s_next = scores(0)
        prev = None
        for u in range(n_units):
            s_cur = s_next
            if u + 1 < n_units:
                s_next = scores(u + 1)
            cur = softmax(u, s_cur)
            if prev is not None:
                accumulate(u - 1, *prev)
            prev = cur
        accumulate(n_units - 1, *prev)

    full = (ki + 1) * tk - 1 <= qi * tq

    @pl.when(full)
    def _():
        body(False)

    @pl.when(jnp.logical_not(full))
    def _():
        if tq == tk:
            diagonal_body()
        else:
            body(True)

    @pl.when(ki == (qi * tq + tq - 1) // tk)
    def _():
        for b in range(bsz):
            for j in range(N_HEADS // 2):
                a0, a1 = acc_ref[b * N_HEADS + 2 * j], acc_ref[b * N_HEADS + 2 * j + 1]
                yy = jnp.concatenate([a0[:HEAD_DIM] * (1.0 / a0[AUG0:AUG0 + 1]),
                                      a1[:HEAD_DIM] * (1.0 / a1[AUG0:AUG0 + 1])], axis=0)
                o_ref[b, :, j * LANES:(j + 1) * LANES] = yy.T.astype(BF16)


def _flash(qa, ka, va, tq, tk):
    b, _, s, _ = ka.shape
    pairs = [(qi, ki) for qi in range(s // tq) for ki in range((qi * tq + tq - 1) // tk + 1)]
    qi_tab = jnp.asarray(np.array([p[0] for p in pairs], np.int32))
    ki_tab = jnp.asarray(np.array([p[1] for p in pairs], np.int32))
    grid_spec = pltpu.PrefetchScalarGridSpec(
        num_scalar_prefetch=2,
        grid=(len(pairs),),
        in_specs=[pl.BlockSpec((b, N_HEADS, LANES, tq), lambda p, qt, kt: (0, 0, 0, qt[p])),
                  pl.BlockSpec((b, N_HEADS, tk, LANES), lambda p, qt, kt: (0, 0, kt[p], 0)),
                  pl.BlockSpec((b, N_HEADS, V_ROWS, tk), lambda p, qt, kt: (0, 0, 0, kt[p]))],
        out_specs=pl.BlockSpec((b, tq, BRANCH_W), lambda p, qt, kt: (0, qt[p], 0)),
        scratch_shapes=[pltpu.VMEM((b * N_HEADS, 1, tq), F32), pltpu.VMEM((b * N_HEADS, V_ROWS, tq), F32)],
    )
    return pl.pallas_call(
        functools.partial(_flash_kernel, tq=tq, tk=tk, bsz=b),
        grid_spec=grid_spec,
        out_shape=jax.ShapeDtypeStruct((b, s, BRANCH_W), BF16),
        compiler_params=pltpu.CompilerParams(dimension_semantics=("arbitrary",), vmem_limit_bytes=VMEM_LIMIT),
        name="prompt_flash",
    )(qi_tab, ki_tab, qa, ka, va)


def _merge_kernel(h_ref, ya_ref, sga_ref, sma_ref, mbc_ref, wa_ref, wo_ref, g_ref, b_ref, o_ref, *, alpha):
    oa = _dot((ya_ref[...].astype(F32) * sga_ref[...].astype(F32)).astype(BF16), wa_ref[...])
    m = sma_ref[...].astype(F32) * oa + mbc_ref[...].astype(F32)
    x = alpha * h_ref[...] + _dot(m.astype(BF16), wo_ref[...])
    o_ref[...] = _layer_norm(x, g_ref[...], b_ref[...])


def _merge(h, ya, sga, sma, mbc, wa, wo, g, b, alpha, tm):
    n, d = h.shape
    rows = lambda w: pl.BlockSpec((tm, w), lambda i: (i, 0))
    const = lambda shape: pl.BlockSpec(shape, lambda i: (0,) * len(shape))
    return pl.pallas_call(
        functools.partial(_merge_kernel, alpha=alpha),
        grid=(n // tm,),
        in_specs=[rows(d), rows(BRANCH_W), rows(BRANCH_W), rows(d), rows(d),
                  const((BRANCH_W, d)), const((d, d)), const((1, d)), const((1, d))],
        out_specs=rows(d),
        out_shape=jax.ShapeDtypeStruct((n, d), F32),
        compiler_params=pltpu.CompilerParams(dimension_semantics=("arbitrary",), vmem_limit_bytes=VMEM_LIMIT),
        name="merge",
    )(h, ya, sga, sma, mbc, wa, wo, g, b)


def _sample_proj_kernel(h_ref, w1_ref, w1t_ref, bf_ref, cw_ref, lvg_ref, lvb_ref, wst_ref, bsr_ref, wb_ref, wc_ref,
                        zp1_ref, zp2_ref,
                        q_ref, k_ref, v_ref, lf_ref, cnt_ref, z_ref, vn_ref, sga_ref, sma_ref, mbc_ref,
                        p_ref, zbuf_ref, *, tn, n_steps, t_new, d_model):
    off, _ = _offsets(d_model)
    j = pl.program_id(0)
    ns = h_ref.shape[0]
    hb = h_ref[...].astype(BF16)
    col = pl.multiple_of(j * tn, LANES)
    p_ref[:, pl.ds(col, tn)] = _dot(hb, w1_ref[...])

    @pl.when(j == n_steps - 1)
    def _():
        def proj(name, width=BRANCH_W):
            return p_ref[:, off[name]:off[name] + width]

        qv = _dot_nt(hb, w1t_ref[...])
        q_ref[...] = qv[:, 0:BRANCH_W]
        v_ref[...] = qv[:, BRANCH_W:2 * BRANCH_W]
        k_ref[...] = proj("k")
        logf = _log_sigmoid(qv[:, 2 * BRANCH_W:2 * BRANCH_W + FORGET_PAD] + bf_ref[...])
        lf_ref[...] = logf[:, :N_HEADS]
        lt = logf.T
        tpos = lax.broadcasted_iota(jnp.int32, lt.shape, 1) % t_new
        cn = lt
        shift = 1
        while shift < t_new:
            cn = cn + jnp.where(tpos >= shift, pltpu.roll(cn, shift, axis=1), 0.0)
            shift *= 2
        cnt_ref[...] = cn[0:N_HEADS, :]

        sga_ref[...] = _silu(proj("ga")).astype(BF16)

        z = proj("cc") * proj("ch")
        z_ref[...] = z
        zbuf_ref[0:SUBLANES, :] = jnp.zeros((SUBLANES, BRANCH_W), F32)
        zbuf_ref[SUBLANES:SUBLANES + ns, :] = z
        trow = lax.broadcasted_iota(jnp.int32, (ns, BRANCH_W), 0) % t_new
        z1 = jnp.where(trow < 1, zp1_ref[...], zbuf_ref[SUBLANES - 1:SUBLANES - 1 + ns, :])
        z2 = jnp.where(trow < 2, zp2_ref[...], zbuf_ref[SUBLANES - 2:SUBLANES - 2 + ns, :])
        ob = _dot(_conv_gate(z, z1, z2, proj("cb"), proj("gb"), cw_ref), wb_ref[...])

        vn = _layer_norm(proj("vc"), lvg_ref[...], lvb_ref[...])
        vn_ref[...] = vn
        vnb = vn.astype(BF16)
        r = lax.broadcasted_iota(jnp.int32, (ns, ns), 0)
        c = lax.broadcasted_iota(jnp.int32, (ns, ns), 1)
        keep = (r // t_new == c // t_new) & (c % t_new <= r % t_new)
        s = jnp.concatenate(
            [_dot(jnp.where(keep, wst_ref[g], 0.0).astype(BF16), vnb[:, g * GROUP_DIM:(g + 1) * GROUP_DIM])
             + bsr_ref[:, g:g + 1] for g in range(N_GROUPS)], axis=1)
        oc = _dot((proj("u") * s * _silu(proj("gc"))).astype(BF16), wc_ref[...])

        sma_ref[...] = _sigmoid(proj("ma", d_model)).astype(BF16)
        mbc_ref[...] = (_sigmoid(proj("mb", d_model)) * ob + _sigmoid(proj("mc", d_model)) * oc).astype(BF16)


def _sample_proj(layer, h, w1, w1t, bf, cw, lvg, lvb, wst, bsr, wb, wc, zp1, zp2, t_new):
    ns, d = h.shape
    wtot = w1.shape[2]
    tn = W1_ALIGN
    n_steps = wtot // tn
    const = lambda shape: pl.BlockSpec(shape, lambda j: (0,) * len(shape))
    f32 = lambda w: jax.ShapeDtypeStruct((ns, w), F32)
    bf16 = lambda w: jax.ShapeDtypeStruct((ns, w), BF16)
    return pl.pallas_call(
        functools.partial(_sample_proj_kernel, tn=tn, n_steps=n_steps, t_new=t_new, d_model=d),
        grid=(n_steps,),
        in_specs=[const((ns, d)), pl.BlockSpec((None, d, tn), lambda j: (layer, 0, j)),
                  pl.BlockSpec((None, W1T_ROWS, d), lambda j: (layer, 0, 0)), const((1, FORGET_PAD)),
                  const((CONV_WIDTH, BRANCH_W)), const((1, BRANCH_W)), const((1, BRANCH_W)),
                  const((N_GROUPS, ns, ns)), const((ns, N_GROUPS)), const((BRANCH_W, d)), const((BRANCH_W, d)),
                  const((ns, BRANCH_W)), const((ns, BRANCH_W))],
        out_specs=[const((ns, BRANCH_W)), const((ns, BRANCH_W)), const((ns, BRANCH_W)), const((ns, N_HEADS)),
                   const((N_HEADS, ns)), const((ns, BRANCH_W)), const((ns, BRANCH_W)),
                   const((ns, BRANCH_W)), const((ns, d)), const((ns, d))],
        out_shape=[f32(BRANCH_W), f32(BRANCH_W), f32(BRANCH_W), f32(N_HEADS),
                   jax.ShapeDtypeStruct((N_HEADS, ns), F32), f32(BRANCH_W), f32(BRANCH_W),
                   bf16(BRANCH_W), bf16(d), bf16(d)],
        scratch_shapes=[pltpu.VMEM((ns, wtot), F32), pltpu.VMEM((ns + SUBLANES, BRANCH_W), F32)],
        compiler_params=pltpu.CompilerParams(dimension_semantics=("arbitrary",), vmem_limit_bytes=VMEM_LIMIT),
        name="sample_proj",
    )(h, w1, w1t, bf, cw, lvg, lvb, wst, bsr, wb, wc, zp1, zp2)


def _sample_attn_kernel(rows_ref, q_ref, kn_ref, vn_ref, cn_ref, ck_hbm, cv_hbm, clf_hbm, o_ref,
                        kbuf, vbuf, lbuf, sem, m_ref, l_ref, acc_ref, car_ref, *, pp, t_new):
    step = pl.program_id(1)
    n_steps = pl.num_programs(1)
    g = pl.program_id(0) * n_steps + step
    total = pl.num_programs(0) * n_steps
    slot = g % 2
    tpad = q_ref.shape[2]

    def page_copies(gg, sl):
        out = []
        for i in range(pp):
            row = rows_ref[gg * pp + i]
            out.append(pltpu.make_async_copy(ck_hbm.at[row], kbuf.at[sl, i], sem.at[sl, 0]))
            out.append(pltpu.make_async_copy(cv_hbm.at[row], vbuf.at[sl, i], sem.at[sl, 1]))
            out.append(pltpu.make_async_copy(clf_hbm.at[row], lbuf.at[sl, i], sem.at[sl, 2]))
        return out

    @pl.when(g == 0)
    def _():
        for c in page_copies(0, 0):
            c.start()

    @pl.when(g + 1 < total)
    def _():
        for c in page_copies(g + 1, 1 - slot):
            c.start()

    @pl.when(step == 0)
    def _():
        car_ref[...] = jnp.zeros_like(car_ref)
        q3 = q_ref[0] * SCALE
        trow = lax.broadcasted_iota(jnp.int32, (N_HEADS, tpad, 1), 1)
        ss = []
        for t in range(t_new):
            st = jnp.sum(q3 * kn_ref[0, :, t:t + 1, :], axis=2, keepdims=True) - cn_ref[0, :, t:t + 1][:, :, None]
            ss.append(jnp.where(trow >= t, st, -jnp.inf))
        m = ss[0]
        for st in ss[1:]:
            m = jnp.maximum(m, st)
        l = jnp.zeros((N_HEADS, tpad, 1), F32)
        acc = jnp.zeros((N_HEADS, tpad, HEAD_DIM), F32)
        for t in range(t_new):
            pt = jnp.exp(ss[t] - m)
            l = l + pt
            acc = acc + pt * vn_ref[0, :, t:t + 1, :]
        m_ref[...] = m
        l_ref[...] = l
        acc_ref[...] = acc

    for c in page_copies(g, slot):
        c.wait()

    ci = lax.broadcasted_iota(jnp.int32, (LANES, 2 * LANES), 0)
    cj = lax.broadcasted_iota(jnp.int32, (LANES, 2 * LANES), 1)
    usum = jnp.where((cj >= LANES) | (ci > cj), 1.0, 0.0).astype(BF16)
    carry = car_ref[...]
    bias = []
    for i in range(pp):
        r = jnp.zeros((N_HEADS, 2 * LANES), F32)
        for piece in _split3(lbuf[slot, i]):
            r = r + _dot(piece, usum)
        bias.append(r[:, :LANES] + carry)
        carry = carry + r[:, LANES:]
    car_ref[...] = carry
    bias = jnp.concatenate(bias, axis=1)

    ss = []
    for h in range(N_HEADS):
        qh = (q_ref[0, h] * SCALE).astype(BF16)
        kt = jnp.concatenate([kbuf[slot, i, h * HEAD_DIM:(h + 1) * HEAD_DIM, :] for i in range(pp)],
                             axis=1).astype(BF16)
        ss.append(_dot(qh, kt) + bias[h:h + 1, :])
    ps, al = [], []
    for h in range(N_HEADS):
        m_old = m_ref[h]
        m_new = jnp.maximum(m_old, jnp.max(ss[h], axis=1, keepdims=True))
        a = jnp.exp(m_old - m_new)
        p = jnp.exp(ss[h] - m_new)
        l_ref[h] = a * l_ref[h] + jnp.sum(p, axis=1, keepdims=True)
        m_ref[h] = m_new
        ps.append(p.astype(BF16))
        al.append(a)
    for h in range(N_HEADS):
        vt = jnp.concatenate([vbuf[slot, i, h * HEAD_DIM:(h + 1) * HEAD_DIM, :] for i in range(pp)],
                             axis=1).astype(BF16)
        acc_ref[h] = al[h] * acc_ref[h] + _dot_nt(ps[h], vt)

    @pl.when(step == n_steps - 1)
    def _():
        for h in range(N_HEADS):
            o_ref[0, h] = acc_ref[h] * (1.0 / l_ref[h])


def _sample_attn(rows, q, kn, vn, cn, cache_kt, cache_vt, cache_lft, n_pages, t_new):
    nb, tpad = q.shape[0], q.shape[2]
    width, page = cache_kt.shape[1], cache_kt.shape[2]
    pp = PAGES_PER_STEP if n_pages % PAGES_PER_STEP == 0 else n_pages
    n_steps = n_pages // pp
    per_seq = lambda shape: pl.BlockSpec((1,) + shape, lambda b, s, rref: (b,) + (0,) * len(shape))
    in_hbm = pl.BlockSpec(memory_space=pl.ANY)
    grid_spec = pltpu.PrefetchScalarGridSpec(
        num_scalar_prefetch=1,
        grid=(nb, n_steps),
        in_specs=[per_seq((N_HEADS, tpad, HEAD_DIM))] * 3 + [per_seq((N_HEADS, LANES))] + [in_hbm] * 3,
        out_specs=per_seq((N_HEADS, tpad, HEAD_DIM)),
        scratch_shapes=[pltpu.VMEM((2, pp, width, page), F32), pltpu.VMEM((2, pp, width, page), F32),
                        pltpu.VMEM((2, pp, N_HEADS, page), F32), pltpu.SemaphoreType.DMA((2, 3)),
                        pltpu.VMEM((N_HEADS, tpad, 1), F32), pltpu.VMEM((N_HEADS, tpad, 1), F32),
                        pltpu.VMEM((N_HEADS, tpad, HEAD_DIM), F32), pltpu.VMEM((N_HEADS, LANES), F32)],
    )
    return pl.pallas_call(
        functools.partial(_sample_attn_kernel, pp=pp, t_new=t_new),
        grid_spec=grid_spec,
        out_shape=jax.ShapeDtypeStruct((nb, N_HEADS, tpad, HEAD_DIM), F32),
        compiler_params=pltpu.CompilerParams(dimension_semantics=("arbitrary", "arbitrary"),
                                             vmem_limit_bytes=VMEM_LIMIT),
        name="sample_attn",
    )(rows, q, kn, vn, cn, cache_kt, cache_vt, cache_lft)


def _pick_tile(n, pref):
    t = pref
    while n % t:
        t //= 2
    return t


def kernel(x_prompt, x_sample, cache_k, cache_v, cache_logf, state_conv, page_table, ln_in_g, ln_in_b, w_in, b_f,
           conv_w, ln_v_g, ln_v_b, w_s, b_s, w_a_out, w_b_out, w_c_out, w_o, ln_g, ln_b):
    depth, d = w_in.shape[0], w_in.shape[1]
    bsz, seq, _ = x_prompt.shape
    nb, t_new, _ = x_sample.shape
    ns = nb * t_new
    assert seq % CHUNK == 0 and t_new <= SUBLANES and page_table.shape[0] == nb
    alpha = (2.0 * depth) ** 0.25
    tm = _pick_tile(seq, 512)
    tq = _pick_tile(seq, 512)

    _, wtot = _offsets(d)
    q0, k0, v0, f0, g0 = 0, BRANCH_W, 2 * BRANCH_W, 3 * BRANCH_W, 3 * BRANCH_W + N_HEADS
    w_int = jnp.swapaxes(w_in, 1, 2)
    w1, w1t = _prep_weights(w_int, wtot)
    b_f_pad = jnp.pad(b_f, ((0, 0), (0, FORGET_PAD - N_HEADS)))
    bfr = b_f_pad.reshape(depth, 1, FORGET_PAD)
    bfc = jnp.broadcast_to(b_f_pad[:, :BF16_ROWS, None], (depth, BF16_ROWS, LANES))
    wa, wb, wc, wo = (w.astype(BF16) for w in (w_a_out, w_b_out, w_c_out, w_o))
    bst = jnp.swapaxes(b_s, 1, 2)
    wst = jnp.tile(w_s[:, :, :t_new, :t_new], (1, 1, ns // t_new, ns // t_new))
    bsr = jnp.tile(jnp.swapaxes(b_s[:, :, :t_new], 1, 2), (1, ns // t_new, 1))
    pool, page = cache_k.shape[1], cache_k.shape[2]
    n_pages = page_table.shape[1]
    cache_kt = jnp.transpose(cache_k, (0, 1, 3, 4, 2)).reshape(depth * pool, BRANCH_W, page)
    cache_vt = jnp.transpose(cache_v, (0, 1, 3, 4, 2)).reshape(depth * pool, BRANCH_W, page)
    cache_lft = jnp.swapaxes(cache_logf, 2, 3).reshape(depth * pool, N_HEADS, page)
    pages_latest_first = page_table[:, ::-1].reshape(-1)
    zp1 = jnp.concatenate([state_conv[:, :, 1:2], jnp.zeros((depth, nb, t_new - 1, BRANCH_W), F32)], axis=2)
    zp2 = jnp.concatenate([state_conv, jnp.zeros((depth, nb, t_new - 2, BRANCH_W), F32)], axis=2)
    zp1, zp2 = zp1.reshape(depth, ns, BRANCH_W), zp2.reshape(depth, ns, BRANCH_W)

    hp = x_prompt
    hs = _ln_rows(x_sample.reshape(ns, d), ln_in_g, ln_in_b, ns)

    outs = {n: [] for n in ("lp", "cp", "ks", "vs", "ls", "cs", "us")}
    kv_all = (jnp.zeros((depth, bsz, seq, BRANCH_W), F32), jnp.zeros((depth, bsz, BRANCH_W, seq), F32))
    for l in range(depth):
        row1 = lambda a: a[l].reshape(1, -1)
        qa, ka, va, ko_all, vo_all, lf, tail, sga, sma, mbc, *hn = _prompt_proj(
            l, depth, kv_all, hp, w1, w1t, bfc[l], conv_w[l], row1(ln_v_g), row1(ln_v_b), w_s[l], bst[l],
            wb[l], wc[l], tm, ln_in=(ln_in_g, ln_in_b) if l == 0 else None)
        if hn:
            hp = hn[0]
        kv_all = (ko_all, vo_all)
        ya = _flash(qa, ka, va, tq, tq)
        flat = lambda a: a.reshape(bsz * seq, a.shape[-1])
        hp = _merge(flat(hp), flat(ya), flat(sga), flat(sma), flat(mbc), wa[l], wo[l], row1(ln_g), row1(ln_b),
                    alpha, tm).reshape(bsz, seq, d)
        outs["lp"].append(lf); outs["cp"].append(tail)
        qs, ks, vs, lfs, cnt, zs, vns, sga, sma, mbc = _sample_proj(
            l, hs, w1, w1t, bfr[l], conv_w[l], row1(ln_v_g), row1(ln_v_b), wst[l], bsr[l], wb[l], wc[l],
            zp1[l], zp2[l], t_new)
        head_major = lambda a: jnp.pad(jnp.transpose(a.reshape(nb, t_new, N_HEADS, HEAD_DIM), (0, 2, 1, 3)),
                                       ((0, 0), (0, 0), (0, SUBLANES - t_new), (0, 0)))
        cn = jnp.pad(jnp.transpose(cnt.reshape(N_HEADS, nb, t_new), (1, 0, 2)), ((0, 0), (0, 0), (0, LANES - t_new)))
        yas = _sample_attn(pages_latest_first + l * pool, head_major(qs), head_major(ks), head_major(vs),
                           cn, cache_kt, cache_vt, cache_lft, n_pages, t_new)
        yas = jnp.transpose(yas[:, :, :t_new], (0, 2, 1, 3)).reshape(ns, BRANCH_W)
        hs = _merge(hs, yas.astype(BF16), sga, sma, mbc, wa[l], wo[l], row1(ln_g), row1(ln_b), alpha, ns)
        outs["ks"].append(ks); outs["vs"].append(vs); outs["ls"].append(lfs)
        outs["cs"].append(zs.reshape(nb, t_new, BRANCH_W)[:, t_new - (CONV_WIDTH - 1):])
        outs["us"].append(vns)

    st = {n: jnp.stack(v) for n, v in outs.items()}
    ko_all, vo_all = kv_all
    new_v_prompt = jnp.transpose(vo_all.reshape(depth, bsz, N_HEADS, HEAD_DIM, seq), (0, 1, 4, 2, 3))
    return (hp, hs.reshape(nb, t_new, d),
            ko_all.reshape(depth, bsz, seq, N_HEADS, HEAD_DIM), new_v_prompt,
            jnp.swapaxes(st["lp"], 2, 3), st["cp"],
            st["ks"].reshape(depth, nb, t_new, N_HEADS, HEAD_DIM), st["vs"].reshape(depth, nb, t_new, N_HEADS, HEAD_DIM),
            st["ls"].reshape(depth, nb, t_new, N_HEADS), st["cs"], st["us"].reshape(depth, nb, t_new, BRANCH_W))
```
